```python
import math, functools
import jax, jax.numpy as jnp
from jax import lax
import numpy as np

D_MODEL = 1024
BATCH = 8
SEQ = 2048
DEPTH = 1
DEC_BATCH = 128
DEC_SEQ = 8
PAST_LEN = 16384
PAGE_SIZE = 128

D_MIX = D_MODEL
D_LRU = D_MIX // 2
D_POOL = D_MIX - D_LRU
N_LRU_HEADS = 8
LRU_HEAD_DIM = D_LRU // N_LRU_HEADS
LRU_CONV = 4
LRU_C = 8.0
POOL_WINDOWS = (2, 4, 8, 16)
N_POOL_GROUPS = len(POOL_WINDOWS)
POOL_GROUP_DIM = D_POOL // N_POOL_GROUPS
POOL_BUF = max(POOL_WINDOWS) - 1
D_IN = 2 * D_LRU + D_POOL
D_FF = 3 * D_MODEL
FFN_CONV = 3
EPS = 1e-6

kernel_name = "hymba_style_rglru_pool_convffn_step"


def rmsnorm(x, g):
    xf = x.astype(jnp.float32)
    y = xf * lax.rsqrt(jnp.mean(xf * xf, axis=-1, keepdims=True) + EPS)
    return (y * g.astype(jnp.float32)).astype(x.dtype)


def causal_dwconv(x, buf, w, b):
    k = w.shape[0]
    s = x.shape[1]
    xp = jnp.concatenate([buf.astype(x.dtype), x], axis=1)
    y = xp[:, 0:s] * w[0]
    for j in range(1, k):
        y = y + xp[:, j:j + s] * w[j]
    y = y + b
    new_buf = xp[:, xp.shape[1] - (k - 1):]
    return y, new_buf.astype(buf.dtype)


def rg_lru(x, h0, wa, ba, wx, bx, lam):
    bsz, s, _ = x.shape
    xh = x.reshape(bsz, s, N_LRU_HEADS, LRU_HEAD_DIM)
    r = jax.nn.sigmoid(jnp.einsum('bshi,hij->bshj', xh, wa).reshape(bsz, s, D_LRU) + ba)
    i = jax.nn.sigmoid(jnp.einsum('bshi,hij->bshj', xh, wx).reshape(bsz, s, D_LRU) + bx)
    log_a = -LRU_C * r.astype(jnp.float32) * jax.nn.softplus(-lam.astype(jnp.float32))
    a = jnp.exp(log_a)
    mult = jnp.sqrt(-jnp.expm1(2.0 * log_a))
    bterm = mult * (i * x).astype(jnp.float32)
    bterm = bterm.at[:, 0].add(a[:, 0] * h0.astype(jnp.float32))

    def combine(left, right):
        al, bl = left
        ar, br = right
        return al * ar, ar * bl + br

    _, h = lax.associative_scan(combine, (a, bterm), axis=1)
    return h.astype(x.dtype), h[:, -1].astype(h0.dtype)


def pool_mixer(x, buf, wg, scale, start):
    bsz, s, _ = x.shape
    xp = jnp.concatenate([buf.astype(x.dtype), x], axis=1).astype(jnp.float32)
    cs = jnp.concatenate([jnp.zeros((bsz, 1, D_POOL), jnp.float32), jnp.cumsum(xp, axis=1)], axis=1)
    pos = start + jnp.arange(s)
    xt = xp[:, POOL_BUF:]
    outs = []
    for g, w in enumerate(POOL_WINDOWS):
        c0, c1 = g * POOL_GROUP_DIM, (g + 1) * POOL_GROUP_DIM
        hi = cs[:, POOL_BUF + 1:POOL_BUF + 1 + s, c0:c1]
        lo = cs[:, POOL_BUF + 1 - w:POOL_BUF + 1 - w + s, c0:c1]
        cnt = jnp.minimum(w, pos + 1).astype(jnp.float32)[None, :, None]
        outs.append((hi - lo) / cnt - xt[:, :, c0:c1])
    pooled = jnp.stack(outs, axis=2)
    y = jnp.einsum('bsgi,gij->bsgj', pooled, wg.astype(jnp.float32)).reshape(bsz, s, D_POOL)
    y = y * scale.astype(jnp.float32)
    new_buf = xp[:, xp.shape[1] - POOL_BUF:]
    return y.astype(x.dtype), new_buf.astype(buf.dtype)


def layer(x, lru_buf, h0, pool_buf, ffn_buf, start,
          norm1_g, w_in, lru_conv_w, lru_conv_b, lru_wa, lru_ba, lru_wx, lru_bx, lru_lambda,
          pool_w, pool_scale, w_out, norm2_g, ffn_up, ffn_conv_w, ffn_conv_b, ffn_down):
    h = rmsnorm(x, norm1_g)
    z = jnp.einsum('bsd,de->bse', h, w_in)
    zx = z[..., :D_LRU]
    zg = z[..., D_LRU:2 * D_LRU]
    zp = z[..., 2 * D_LRU:]
    xc, new_lru_buf = causal_dwconv(zx, lru_buf, lru_conv_w, lru_conv_b)
    hl, h_last = rg_lru(xc, h0, lru_wa, lru_ba, lru_wx, lru_bx, lru_lambda)
    lru_out = hl * jax.nn.gelu(zg)
    pool_out, new_pool_buf = pool_mixer(zp, pool_buf, pool_w, pool_scale, start)
    mixed = jnp.concatenate([lru_out, pool_out], axis=-1)
    x = x + jnp.einsum('bse,ed->bsd', mixed, w_out)
    h2 = rmsnorm(x, norm2_g)
    u = jnp.einsum('bsd,df->bsf', h2, ffn_up)
    uc, new_ffn_buf = causal_dwconv(u, ffn_buf, ffn_conv_w, ffn_conv_b)
    gate = uc[..., :D_FF]
    val = uc[..., D_FF:]
    x = x + jnp.einsum('bsf,fd->bsd', jax.nn.gelu(gate) * val, ffn_down)
    return x, new_lru_buf, h_last, new_pool_buf, new_ffn_buf


def setup_inputs(seed: int = 0) -> dict:
    key = jax.random.key(seed)
    ks = jax.random.split(key, 24)
    f32 = jnp.float32
    nrm = lambda k, shape, sc: jax.random.normal(k, shape, f32) * sc
    a0 = jax.random.uniform(ks[13], (DEPTH, D_LRU), f32, minval=0.9, maxval=0.999)
    return {
        "x_prompt": nrm(ks[0], (BATCH, SEQ, D_MODEL), 1.0),
        "x_sample": nrm(ks[1], (DEC_BATCH, DEC_SEQ, D_MODEL), 1.0),
        "state_lru_conv": nrm(ks[2], (DEPTH, DEC_BATCH, LRU_CONV - 1, D_LRU), 1.0),
        "state_lru_h": nrm(ks[3], (DEPTH, DEC_BATCH, D_LRU), 0.5),
        "state_pool": nrm(ks[4], (DEPTH, DEC_BATCH, POOL_BUF, D_POOL), 1.0),
        "state_ffn_conv": nrm(ks[5], (DEPTH, DEC_BATCH, FFN_CONV - 1, 2 * D_FF), 1.0),
        "norm1_g": 1.0 + nrm(ks[6], (DEPTH, D_MODEL), 0.05),
        "w_in": nrm(ks[7], (DEPTH, D_MODEL, D_IN), D_MODEL ** -0.5),
        "lru_conv_w": nrm(ks[8], (DEPTH, LRU_CONV, D_LRU), LRU_CONV ** -0.5),
        "lru_conv_b": nrm(ks[9], (DEPTH, D_LRU), 0.01),
        "lru_wa": nrm(ks[10], (DEPTH, N_LRU_HEADS, LRU_HEAD_DIM, LRU_HEAD_DIM), LRU_HEAD_DIM ** -0.5),
        "lru_ba": nrm(ks[11], (DEPTH, D_LRU), 0.01),
        "lru_wx": nrm(ks[12], (DEPTH, N_LRU_HEADS, LRU_HEAD_DIM, LRU_HEAD_DIM), LRU_HEAD_DIM ** -0.5),
        "lru_bx": nrm(ks[14], (DEPTH, D_LRU), 0.01),
        "lru_lambda": jnp.log(a0) - jnp.log1p(-a0),
        "pool_w": nrm(ks[15], (DEPTH, N_POOL_GROUPS, POOL_GROUP_DIM, POOL_GROUP_DIM), POOL_GROUP_DIM ** -0.5),
        "pool_scale": 0.5 + nrm(ks[16], (DEPTH, D_POOL), 0.05),
        "w_out": nrm(ks[17], (DEPTH, D_MIX, D_MODEL), D_MIX ** -0.5),
        "norm2_g": 1.0 + nrm(ks[18], (DEPTH, D_MODEL), 0.05),
        "ffn_up": nrm(ks[19], (DEPTH, D_MODEL, 2 * D_FF), D_MODEL ** -0.5),
        "ffn_conv_w": nrm(ks[20], (DEPTH, FFN_CONV, 2 * D_FF), FFN_CONV ** -0.5),
        "ffn_conv_b": nrm(ks[21], (DEPTH, 2 * D_FF), 0.01),
        "ffn_down": nrm(ks[22], (DEPTH, D_FF, D_MODEL), D_FF ** -0.5),
        "final_g": 1.0 + nrm(ks[23], (D_MODEL,), 0.05),
    }


def reference(x_prompt, x_sample, state_lru_conv, state_lru_h, state_pool, state_ffn_conv,
              norm1_g, w_in, lru_conv_w, lru_conv_b, lru_wa, lru_ba, lru_wx, lru_bx, lru_lambda,
              pool_w, pool_scale, w_out, norm2_g, ffn_up, ffn_conv_w, ffn_conv_b, ffn_down, final_g):
    dt = x_prompt.dtype
    xp, xs = x_prompt, x_sample
    p_lc, p_h, p_pb, p_fb = [], [], [], []
    s_lc, s_h, s_pb, s_fb = [], [], [], []
    for l in range(DEPTH):
        params = (norm1_g[l], w_in[l], lru_conv_w[l], lru_conv_b[l], lru_wa[l], lru_ba[l],
                  lru_wx[l], lru_bx[l], lru_lambda[l], pool_w[l], pool_scale[l], w_out[l],
                  norm2_g[l], ffn_up[l], ffn_conv_w[l], ffn_conv_b[l], ffn_down[l])
        xp, a1, a2, a3, a4 = layer(
            xp,
            jnp.zeros((BATCH, LRU_CONV - 1, D_LRU), dt),
            jnp.zeros((BATCH, D_LRU), dt),
            jnp.zeros((BATCH, POOL_BUF, D_POOL), dt),
            jnp.zeros((BATCH, FFN_CONV - 1, 2 * D_FF), dt),
            0, *params)
        p_lc.append(a1); p_h.append(a2); p_pb.append(a3); p_fb.append(a4)
        xs, b1, b2, b3, b4 = layer(
            xs, state_lru_conv[l], state_lru_h[l], state_pool[l], state_ffn_conv[l],
            PAST_LEN, *params)
        s_lc.append(b1); s_h.append(b2); s_pb.append(b3); s_fb.append(b4)
    y_prompt = rmsnorm(xp, final_g)
    y_sample = rmsnorm(xs, final_g)
    return (y_prompt, y_sample,
            jnp.stack(p_lc), jnp.stack(p_h), jnp.stack(p_pb), jnp.stack(p_fb),
            jnp.stack(s_lc), jnp.stack(s_h), jnp.stack(s_pb), jnp.stack(s_fb))
```

```python
import functools

import jax
import jax.numpy as jnp
from jax import lax
from jax.experimental import pallas as pl
from jax.experimental.pallas import tpu as pltpu

D_MODEL = 1024
D_LRU = 512
D_POOL = 512
N_LRU_HEADS = 8
LRU_HEAD_DIM = D_LRU // N_LRU_HEADS
LRU_CONV = 4
LRU_C = 8.0
POOL_WINDOWS = (2, 4, 8, 16)
POOL_GROUP_DIM = D_POOL // len(POOL_WINDOWS)
POOL_BUF = max(POOL_WINDOWS) - 1
D_FF = 3 * D_MODEL
FFN_CONV = 3
EPS = 1e-6
PAST_LEN = 16384

MXU_TILE = 256
LANES = 128
FF_CHUNK = 512
VMEM_LIMIT_BYTES = 58 * 1024 * 1024

BF16 = jnp.bfloat16
F32 = jnp.float32


def _rmsnorm(x, g):
    y = x * lax.rsqrt(jnp.mean(x * x, axis=-1, keepdims=True) + EPS)
    return y * g


def _mm(a, b):
    return jnp.dot(a, b, preferred_element_type=F32)


def _layer_kernel(x_ref, slc_ref, sh_ref, spb_ref, sfb_ref,
                  g1_ref, win_ref, cw_ref, cb_ref, wg_ref, ba_ref, bx_ref, lam_ref,
                  wp_ref, ps_ref, wout_ref, g2_ref, up_ref, fw_ref, fb_ref, down_ref, gf_ref,
                  y_ref, olc_ref, oh_ref, opb_ref, ofb_ref,
                  extx, extp, hst, fcar, extg, extv, abuf, bbuf, hl, hb,
                  *, nb, tt, sequential, start):
    i = pl.program_id(0)
    rows = nb * tt
    lc = (LRU_CONV - 1) * nb
    pc = POOL_BUF * nb
    fc = (FFN_CONV - 1) * nb

    def load_state():
        extx[0:lc] = slc_ref[...]
        extp[0:pc] = spb_ref[...]
        hst[...] = sh_ref[...]
        fcar[...] = sfb_ref[...]

    if sequential:
        pl.when(i == 0)(load_state)
    else:
        load_state()

    x = x_ref[...]
    hb[...] = _rmsnorm(x, g1_ref[...]).astype(BF16)
    extx[lc:lc + rows] = _mm(hb[...], win_ref[:, 0:D_LRU])
    hl[...] = jax.nn.gelu(_mm(hb[...], win_ref[:, D_LRU:2 * D_LRU]))
    extp[pc:pc + rows] = _mm(hb[...], win_ref[:, 2 * D_LRU:])

    xc = cb_ref[...] + cw_ref[0:1, :] * extx[0:rows]
    for j in range(1, LRU_CONV):
        xc = xc + cw_ref[j:j + 1, :] * extx[j * nb:j * nb + rows]
    xcb = xc.astype(BF16)

    softplus_neg_lam = jnp.logaddexp(-lam_ref[...], 0.0)
    for k in range(D_LRU // MXU_TILE):
        cs = slice(k * MXU_TILE, (k + 1) * MXU_TILE)
        pre = _mm(xcb[:, cs], wg_ref[k])
        r = jax.nn.sigmoid(pre[:, :MXU_TILE] + ba_ref[:, cs])
        ig = jax.nn.sigmoid(pre[:, MXU_TILE:] + bx_ref[:, cs])
        log_a = -LRU_C * r * softplus_neg_lam[:, cs]
        a = jnp.exp(log_a)
        mult = jnp.sqrt(jnp.tanh(-log_a) * (1.0 + a * a))
        abuf[:, cs] = a
        bbuf[:, cs] = mult * (ig * xc[:, cs])

    def step(t, h):
        r0 = pl.multiple_of(t * nb, nb)
        h = abuf[pl.ds(r0, nb), :] * h + bbuf[pl.ds(r0, nb), :]
        gated = h * hl[pl.ds(r0, nb), :]
        hl[pl.ds(r0, nb), :] = gated
        return h

    hst[...] = lax.fori_loop(0, tt, step, hst[...], unroll=min(tt, 8))

    t_local = lax.shift_right_logical(lax.broadcasted_iota(jnp.int32, (rows, LANES), 0), nb.bit_length() - 1)
    pos = start + t_local + (i * tt if sequential else 0)
    pooled = []
    for g, w in enumerate(POOL_WINDOWS):
        cs = slice(g * POOL_GROUP_DIM, (g + 1) * POOL_GROUP_DIM)
        cur = extp[pc:pc + rows, cs]
        s = cur
        for j in range(1, w):
            s = s + extp[pc - j * nb:pc - j * nb + rows, cs]
        cnt = jnp.minimum(w, pos + 1).astype(F32)
        pooled.append((s / cnt - cur).astype(BF16))

    x1 = x + _mm(hl[...].astype(BF16), wout_ref[0:D_LRU, :])
    for k in range(D_POOL // MXU_TILE):
        cs = slice(k * MXU_TILE, (k + 1) * MXU_TILE)
        pk = jnp.concatenate(pooled[2 * k:2 * k + 2], axis=-1)
        pool_out = _mm(pk, wp_ref[k]) * ps_ref[:, cs]
        x1 = x1 + _mm(pool_out.astype(BF16), wout_ref[D_LRU + k * MXU_TILE:D_LRU + (k + 1) * MXU_TILE, :])

    hb[...] = _rmsnorm(x1, g2_ref[...]).astype(BF16)
    for c in range(D_FF // FF_CHUNK):
        branches = []
        for ext, off in ((extg, c * FF_CHUNK), (extv, D_FF + c * FF_CHUNK)):
            cs = slice(off, off + FF_CHUNK)
            ext[0:fc] = fcar[:, cs]
            ext[fc:fc + rows] = _mm(hb[...], up_ref[:, cs])
            fcar[:, cs] = ext[rows:rows + fc]
            u = fb_ref[:, cs] + fw_ref[0:1, cs] * ext[0:rows]
            for j in range(1, FFN_CONV):
                u = u + fw_ref[j:j + 1, cs] * ext[j * nb:j * nb + rows]
            branches.append(u)
        act = (jax.nn.gelu(branches[0]) * branches[1]).astype(BF16)
        down_c = _mm(act, down_ref[c * FF_CHUNK:(c + 1) * FF_CHUNK, :])
        ffn = down_c if c == 0 else ffn + down_c

    y_ref[...] = _rmsnorm(x1 + ffn, gf_ref[...])

    olc_ref[...] = extx[rows:rows + lc]
    opb_ref[...] = extp[rows:rows + pc]
    oh_ref[...] = hst[...]
    ofb_ref[...] = fcar[...]
    if sequential:
        extx[0:lc] = extx[rows:rows + lc]
        extp[0:pc] = extp[rows:rows + pc]


def _const_spec(shape):
    nd = len(shape)
    return pl.BlockSpec(shape, lambda i: (0,) * nd, pipeline_mode=pl.Buffered(1))


def _run_layer(x_tm, states, params, *, nb, tt, n_tiles, sequential, start, name):
    rows = nb * tt
    lc, pc, fc = (LRU_CONV - 1) * nb, POOL_BUF * nb, (FFN_CONV - 1) * nb
    if sequential:
        tile_map = lambda i: (0, 0)
        n_state = 1
    else:
        tile_map = lambda i: (i, 0)
        n_state = n_tiles
    state_shapes = [(lc, D_LRU), (nb, D_LRU), (pc, D_POOL), (fc, 2 * D_FF)]
    in_specs = [pl.BlockSpec((rows, D_MODEL), lambda i: (i, 0))]
    in_specs += [pl.BlockSpec(s, tile_map) for s in state_shapes]
    in_specs += [_const_spec(p.shape) for p in params]
    out_specs = [pl.BlockSpec((rows, D_MODEL), lambda i: (i, 0))]
    out_specs += [pl.BlockSpec(s, tile_map) for s in state_shapes]
    out_shape = [jax.ShapeDtypeStruct(x_tm.shape, F32)]
    out_shape += [jax.ShapeDtypeStruct((n_state * s[0], s[1]), F32) for s in state_shapes]
    scratch = [
        pltpu.VMEM((lc + rows, D_LRU), F32),
        pltpu.VMEM((pc + rows, D_POOL), F32),
        pltpu.VMEM((nb, D_LRU), F32),
        pltpu.VMEM((fc, 2 * D_FF), F32),
        pltpu.VMEM((fc + rows, FF_CHUNK), F32),
        pltpu.VMEM((fc + rows, FF_CHUNK), F32),
        pltpu.VMEM((rows, D_LRU), F32),
        pltpu.VMEM((rows, D_LRU), F32),
        pltpu.VMEM((rows, D_LRU), F32),
        pltpu.VMEM((rows, D_MODEL), BF16),
    ]
    body = functools.partial(_layer_kernel, nb=nb, tt=tt, sequential=sequential, start=start)
    return pl.pallas_call(
        body,
        grid=(n_tiles,),
        in_specs=in_specs,
        out_specs=out_specs,
        out_shape=out_shape,
        scratch_shapes=scratch,
        compiler_params=pltpu.CompilerParams(
            dimension_semantics=("arbitrary",), vmem_limit_bytes=VMEM_LIMIT_BYTES),
        name=name,
    )(x_tm, *states, *params)


def _block_diag_tiles(w, per_tile):
    n, d, _ = w.shape
    tiles = []
    for k in range(n // per_tile):
        tiles.append(jax.scipy.linalg.block_diag(*[w[k * per_tile + j] for j in range(per_tile)]))
    return jnp.stack(tiles)


def _to_time_major(a, groups):
    b, k, c = a.shape
    return a.reshape(groups, b // groups, k, c).transpose(0, 2, 1, 3).reshape(b * k, c)


def _from_time_major(a, groups, b, k):
    c = a.shape[-1]
    return a.reshape(groups, k, b // groups, c).transpose(0, 2, 1, 3).reshape(b, k, c)


PROMPT_TT = 64
SAMPLE_GROUPS = 4


def kernel(x_prompt, x_sample, state_lru_conv, state_lru_h, state_pool, state_ffn_conv, norm1_g, w_in, lru_conv_w, lru_conv_b, lru_wa, lru_ba, lru_wx, lru_bx, lru_lambda, pool_w, pool_scale, w_out, norm2_g, ffn_up, ffn_conv_w, ffn_conv_b, ffn_down, final_g):
    depth = w_in.shape[0]
    assert depth == 1
    l = 0
    row = lambda v: v.reshape(1, -1)
    per_tile = MXU_TILE // LRU_HEAD_DIM
    wa_t = _block_diag_tiles(lru_wa[l], per_tile)
    wx_t = _block_diag_tiles(lru_wx[l], per_tile)
    params = (
        row(norm1_g[l]), w_in[l].astype(BF16), lru_conv_w[l], row(lru_conv_b[l]),
        jnp.concatenate([wa_t, wx_t], axis=-1).astype(BF16),
        row(lru_ba[l]), row(lru_bx[l]), row(lru_lambda[l]),
        _block_diag_tiles(pool_w[l], MXU_TILE // POOL_GROUP_DIM).astype(BF16), row(pool_scale[l]),
        w_out[l].astype(BF16), row(norm2_g[l]), ffn_up[l].astype(BF16),
        ffn_conv_w[l], row(ffn_conv_b[l]), ffn_down[l].astype(BF16), row(final_g),
    )

    bp, sp, _ = x_prompt.shape
    zeros = lambda k, c: jnp.zeros((k * bp, c), F32)
    p_states = (zeros(LRU_CONV - 1, D_LRU), zeros(1, D_LRU), zeros(POOL_BUF, D_POOL), zeros(FFN_CONV - 1, 2 * D_FF))
    yp, p_lc, p_h, p_pb, p_fb = _run_layer(
        _to_time_major(x_prompt, 1), p_states, params,
        nb=bp, tt=PROMPT_TT, n_tiles=sp // PROMPT_TT, sequential=True, start=0, name="layer_prompt")

    bs, ss, _ = x_sample.shape
    g = SAMPLE_GROUPS
    s_states = (_to_time_major(state_lru_conv[l], g), state_lru_h[l],
                _to_time_major(state_pool[l], g), _to_time_major(state_ffn_conv[l], g))
    ys, s_lc, s_h, s_pb, s_fb = _run_layer(
        _to_time_major(x_sample, g), s_states, params,
        nb=bs // g, tt=ss, n_tiles=g, sequential=False, start=PAST_LEN, name="layer_sample")

    out = (
        _from_time_major(yp, 1, bp, sp), _from_time_major(ys, g, bs, ss),
        _from_time_major(p_lc, 1, bp, LRU_CONV - 1)[None], p_h[None],
        _from_time_major(p_pb, 1, bp, POOL_BUF)[None], _from_time_major(p_fb, 1, bp, FFN_CONV - 1)[None],
        _from_time_major(s_lc, g, bs, LRU_CONV - 1)[None], s_h[None],
        _from_time_major(s_pb, g, bs, POOL_BUF)[None], _from_time_major(s_fb, g, bs, FFN_CONV - 1)[None],
    )
    return out
```

```python
import functools

import jax
import jax.numpy as jnp
from jax import lax
from jax.experimental import pallas as pl
from jax.experimental.pallas import tpu as pltpu

D_MODEL = 1024
D_LRU = 512
D_POOL = 512
N_LRU_HEADS = 8
LRU_HEAD_DIM = D_LRU // N_LRU_HEADS
LRU_CONV = 4
LRU_C = 8.0
POOL_WINDOWS = (2, 4, 8, 16)
POOL_GROUP_DIM = D_POOL // len(POOL_WINDOWS)
POOL_BUF = max(POOL_WINDOWS) - 1
D_FF = 3 * D_MODEL
FFN_CONV = 3
EPS = 1e-6
PAST_LEN = 16384

MXU_TILE = 256
LANES = 128
SUBLANES = 8
FF_CHUNK = 512
N_CHUNKS = D_FF // FF_CHUNK
CHUNKS_PER_STEP = 2
ARENA_W = 512
VMEM_LIMIT_BYTES = 60 * 1024 * 1024

BF16 = jnp.bfloat16
F32 = jnp.float32

assert D_LRU == D_POOL == FF_CHUNK == ARENA_W
assert N_CHUNKS % CHUNKS_PER_STEP == 0


def _rmsnorm(x, g):
    y = x * lax.rsqrt(jnp.mean(x * x, axis=-1, keepdims=True) + EPS)
    return y * g


def _mm(a, b):
    return jnp.dot(a, b, preferred_element_type=F32)


def _layer_kernel(x_ref, slc_ref, sh_ref, spb_ref, sfb_ref,
                  g1_ref, win_ref, cw_ref, cb_ref, wg_ref, ba_ref, bx_ref, lam_ref,
                  wp_ref, ps_ref, wout_ref, g2_ref, up_ref, fw_ref, fb_ref, down_ref, gf_ref,
                  y_ref, olc_ref, oh_ref, opb_ref, ofb_ref,
                  arena, cx, cp, hst, fcar, hb,
                  *, nb, tt, sequential, start):
    i = pl.program_id(0)
    rows = nb * tt
    lc = (LRU_CONV - 1) * nb
    pc = POOL_BUF * nb
    fc = (FFN_CONV - 1) * nb
    ex0 = 0
    ep0 = ex0 + lc + rows
    hl0 = ep0 + pc + rows
    xc0 = hl0 + rows

    def load_state():
        cx[...] = slc_ref[...]
        cp[...] = spb_ref[...]
        hst[...] = sh_ref[...]
        for k in range(2 * N_CHUNKS):
            fcar[k] = sfb_ref[:, k * FF_CHUNK:(k + 1) * FF_CHUNK]

    if sequential:
        pl.when(i == 0)(load_state)
    else:
        load_state()

    hb[...] = _rmsnorm(x_ref[...], g1_ref[...]).astype(BF16)
    arena[ex0:ex0 + lc] = cx[...]
    arena[ep0:ep0 + pc] = cp[...]
    arena[ex0 + lc:ex0 + lc + rows] = _mm(hb[...], win_ref[:, 0:D_LRU])
    arena[hl0:hl0 + rows] = jax.nn.gelu(_mm(hb[...], win_ref[:, D_LRU:2 * D_LRU]))
    arena[ep0 + pc:ep0 + pc + rows] = _mm(hb[...], win_ref[:, 2 * D_LRU:])
    cx[...] = arena[ex0 + rows:ex0 + rows + lc]
    cp[...] = arena[ep0 + rows:ep0 + rows + pc]

    xc = cb_ref[...] + cw_ref[0:1, :] * arena[ex0:ex0 + rows]
    for j in range(1, LRU_CONV):
        xc = xc + cw_ref[j:j + 1, :] * arena[ex0 + j * nb:ex0 + j * nb + rows]
    arena[xc0:xc0 + rows] = xc

    softplus_neg_lam = jnp.logaddexp(-lam_ref[...], 0.0)
    for k in range(D_LRU // MXU_TILE):
        cs = slice(k * MXU_TILE, (k + 1) * MXU_TILE)
        xck = arena[xc0:xc0 + rows, cs]
        pre = _mm(xck.astype(BF16), wg_ref[k])
        r = jax.nn.sigmoid(pre[:, :MXU_TILE] + ba_ref[:, cs])
        ig = jax.nn.sigmoid(pre[:, MXU_TILE:] + bx_ref[:, cs])
        log_a = -LRU_C * r * softplus_neg_lam[:, cs]
        a = jnp.exp(log_a)
        mult = jnp.sqrt(jnp.tanh(-log_a) * (1.0 + a * a))
        y_ref[:, k * MXU_TILE:(k + 1) * MXU_TILE] = a
        y_ref[:, D_LRU + k * MXU_TILE:D_LRU + (k + 1) * MXU_TILE] = mult * (ig * xck)

    def step(t, h):
        r0 = pl.multiple_of(t * nb, nb)
        h = y_ref[pl.ds(r0, nb), 0:D_LRU] * h + y_ref[pl.ds(r0, nb), D_LRU:2 * D_LRU]
        arena[pl.ds(hl0 + r0, nb), :] = h * arena[pl.ds(hl0 + r0, nb), :]
        return h

    hst[...] = lax.fori_loop(0, tt, step, hst[...], unroll=min(tt, 8))

    t_local = lax.shift_right_logical(lax.broadcasted_iota(jnp.int32, (rows, LANES), 0), nb.bit_length() - 1)
    pos = start + t_local + (i * tt if sequential else 0)
    pooled = []
    for g, w in enumerate(POOL_WINDOWS):
        cs = slice(g * POOL_GROUP_DIM, (g + 1) * POOL_GROUP_DIM)
        cur = arena[ep0 + pc:ep0 + pc + rows, cs]
        s = cur
        for j in range(1, w):
            s = s + arena[ep0 + pc - j * nb:ep0 + pc - j * nb + rows, cs]
        cnt = jnp.minimum(w, pos + 1).astype(F32)
        pooled.append((s / cnt - cur).astype(BF16))

    x1 = x_ref[...] + _mm(arena[hl0:hl0 + rows].astype(BF16), wout_ref[0:D_LRU, :])
    for k in range(D_POOL // MXU_TILE):
        cs = slice(k * MXU_TILE, (k + 1) * MXU_TILE)
        pk = jnp.concatenate(pooled[2 * k:2 * k + 2], axis=-1)
        pool_out = _mm(pk, wp_ref[k]) * ps_ref[:, cs]
        x1 = x1 + _mm(pool_out.astype(BF16), wout_ref[D_LRU + k * MXU_TILE:D_LRU + (k + 1) * MXU_TILE, :])
    y_ref[...] = x1

    hb[...] = _rmsnorm(y_ref[...], g2_ref[...]).astype(BF16)

    def ffn_step(p, carry):
        down = None
        for q in range(CHUNKS_PER_STEP):
            c = p * CHUNKS_PER_STEP + q
            branches = []
            for j in range(2):
                base = (2 * q + j) * (fc + rows)
                idx = j * N_CHUNKS + c
                arena[base:base + fc] = fcar[idx]
                arena[base + fc:base + fc + rows] = _mm(hb[...], up_ref[idx])
                fcar[idx] = arena[base + rows:base + rows + fc]
                w = fw_ref[idx]
                u = fb_ref[idx] + w[0:1, :] * arena[base:base + rows]
                for t in range(1, FFN_CONV):
                    u = u + w[t:t + 1, :] * arena[base + t * nb:base + t * nb + rows]
                branches.append(u)
            act = (jax.nn.gelu(branches[0]) * branches[1]).astype(BF16)
            d = _mm(act, down_ref[c])
            down = d if down is None else down + d
        y_ref[...] += down
        return carry

    lax.fori_loop(0, N_CHUNKS // CHUNKS_PER_STEP, ffn_step, 0)

    y_ref[...] = _rmsnorm(y_ref[...], gf_ref[...])

    olc_ref[...] = cx[...]
    opb_ref[...] = cp[...]
    oh_ref[...] = hst[...]
    for k in range(2 * N_CHUNKS):
        ofb_ref[:, k * FF_CHUNK:(k + 1) * FF_CHUNK] = fcar[k]


def _const_spec(shape):
    nd = len(shape)
    return pl.BlockSpec(shape, lambda i: (0,) * nd, pipeline_mode=pl.Buffered(1))


def _run_layer(x_tm, states, params, *, nb, tt, n_tiles, sequential, start, name):
    rows = nb * tt
    assert nb % SUBLANES == 0 and nb & (nb - 1) == 0
    lc, pc, fc = (LRU_CONV - 1) * nb, POOL_BUF * nb, (FFN_CONV - 1) * nb
    if sequential:
        tile_map = lambda i: (0, 0)
        n_state = 1
    else:
        tile_map = lambda i: (i, 0)
        n_state = n_tiles
    state_shapes = [(lc, D_LRU), (nb, D_LRU), (pc, D_POOL), (fc, 2 * D_FF)]
    in_specs = [pl.BlockSpec((rows, D_MODEL), lambda i: (i, 0))]
    in_specs += [pl.BlockSpec(s, tile_map) for s in state_shapes]
    in_specs += [_const_spec(p.shape) for p in params]
    out_specs = [pl.BlockSpec((rows, D_MODEL), lambda i: (i, 0))]
    out_specs += [pl.BlockSpec(s, tile_map) for s in state_shapes]
    out_shape = [jax.ShapeDtypeStruct(x_tm.shape, F32)]
    out_shape += [jax.ShapeDtypeStruct((n_state * s[0], s[1]), F32) for s in state_shapes]
    arena_rows = max(4 * rows + lc + pc, 2 * CHUNKS_PER_STEP * (fc + rows))
    scratch = [
        pltpu.VMEM((arena_rows, ARENA_W), F32),
        pltpu.VMEM((lc, D_LRU), F32),
        pltpu.VMEM((pc, D_POOL), F32),
        pltpu.VMEM((nb, D_LRU), F32),
        pltpu.VMEM((2 * N_CHUNKS, fc, FF_CHUNK), F32),
        pltpu.VMEM((rows, D_MODEL), BF16),
    ]
    body = functools.partial(_layer_kernel, nb=nb, tt=tt, sequential=sequential, start=start)
    return pl.pallas_call(
        body,
        grid=(n_tiles,),
        in_specs=in_specs,
        out_specs=out_specs,
        out_shape=out_shape,
        scratch_shapes=scratch,
        compiler_params=pltpu.CompilerParams(
            dimension_semantics=("arbitrary",), vmem_limit_bytes=VMEM_LIMIT_BYTES),
        name=name,
    )(x_tm, *states, *params)


def _block_diag_tiles(w, per_tile):
    n, d, _ = w.shape
    tiles = []
    for k in range(n // per_tile):
        tiles.append(jax.scipy.linalg.block_diag(*[w[k * per_tile + j] for j in range(per_tile)]))
    return jnp.stack(tiles)


def _chunk_columns(w):
    k = w.shape[0]
    return w.reshape(k, 2 * N_CHUNKS, FF_CHUNK).transpose(1, 0, 2)


def _to_time_major(a, groups):
    b, k, c = a.shape
    return a.reshape(groups, b // groups, k, c).transpose(0, 2, 1, 3).reshape(b * k, c)


def _from_time_major(a, groups, b, k):
    c = a.shape[-1]
    return a.reshape(groups, k, b // groups, c).transpose(0, 2, 1, 3).reshape(b, k, c)


PROMPT_TT = 64
SAMPLE_GROUPS = 4


def kernel(x_prompt, x_sample, state_lru_conv, state_lru_h, state_pool, state_ffn_conv, norm1_g, w_in, lru_conv_w, lru_conv_b, lru_wa, lru_ba, lru_wx, lru_bx, lru_lambda, pool_w, pool_scale, w_out, norm2_g, ffn_up, ffn_conv_w, ffn_conv_b, ffn_down, final_g):
    depth = w_in.shape[0]
    assert depth == 1
    l = 0
    row = lambda v: v.reshape(1, -1)
    per_tile = MXU_TILE // LRU_HEAD_DIM
    wa_t = _block_diag_tiles(lru_wa[l], per_tile)
    wx_t = _block_diag_tiles(lru_wx[l], per_tile)
    params = (
        row(norm1_g[l]), w_in[l].astype(BF16), lru_conv_w[l], row(lru_conv_b[l]),
        jnp.concatenate([wa_t, wx_t], axis=-1).astype(BF16),
        row(lru_ba[l]), row(lru_bx[l]), row(lru_lambda[l]),
        _block_diag_tiles(pool_w[l], MXU_TILE // POOL_GROUP_DIM).astype(BF16), row(pool_scale[l]),
        w_out[l].astype(BF16), row(norm2_g[l]), _chunk_columns(ffn_up[l].astype(BF16)),
        _chunk_columns(ffn_conv_w[l]), _chunk_columns(row(ffn_conv_b[l])),
        ffn_down[l].astype(BF16).reshape(N_CHUNKS, FF_CHUNK, D_MODEL), row(final_g),
    )

    bp, sp, _ = x_prompt.shape
    zeros = lambda k, c: jnp.zeros((k * bp, c), F32)
    p_states = (zeros(LRU_CONV - 1, D_LRU), zeros(1, D_LRU), zeros(POOL_BUF, D_POOL), zeros(FFN_CONV - 1, 2 * D_FF))
    yp, p_lc, p_h, p_pb, p_fb = _run_layer(
        _to_time_major(x_prompt, 1), p_states, params,
        nb=bp, tt=PROMPT_TT, n_tiles=sp // PROMPT_TT, sequential=True, start=0, name="layer_prompt")

    bs, ss, _ = x_sample.shape
    g = SAMPLE_GROUPS
    s_states = (_to_time_major(state_lru_conv[l], g), state_lru_h[l],
                _to_time_major(state_pool[l], g), _to_time_major(state_ffn_conv[l], g))
    ys, s_lc, s_h, s_pb, s_fb = _run_layer(
        _to_time_major(x_sample, g), s_states, params,
        nb=bs // g, tt=ss, n_tiles=g, sequential=False, start=PAST_LEN, name="layer_sample")

    out = (
        _from_time_major(yp, 1, bp, sp), _from_time_major(ys, g, bs, ss),
        _from_time_major(p_lc, 1, bp, LRU_CONV - 1)[None], p_h[None],
        _from_time_major(p_pb, 1, bp, POOL_BUF)[None], _from_time_major(p_fb, 1, bp, FFN_CONV - 1)[None],
        _from_time_major(s_lc, g, bs, LRU_CONV - 1)[None], s_h[None],
        _from_time_major(s_pb, g, bs, POOL_BUF)[None], _from_time_major(s_fb, g, bs, FFN_CONV - 1)[None],
    )
    return out
```

```python
import functools

import jax
import jax.numpy as jnp
from jax import lax
from jax.experimental import pallas as pl
from jax.experimental.pallas import tpu as pltpu

D_MODEL = 1024
D_LRU = 512
D_POOL = 512
N_LRU_HEADS = 8
LRU_HEAD_DIM = D_LRU // N_LRU_HEADS
LRU_CONV = 4
LRU_C = 8.0
POOL_WINDOWS = (2, 4, 8, 16)
POOL_GROUP_DIM = D_POOL // len(POOL_WINDOWS)
POOL_BUF = max(POOL_WINDOWS) - 1
D_FF = 3 * D_MODEL
FFN_CONV = 3
EPS = 1e-6
PAST_LEN = 16384

MXU_TILE = 256
LANES = 128
SUBLANES = 8
FF_CHUNK = 512
N_CHUNKS = D_FF // FF_CHUNK
CHUNKS_PER_STEP = 2
ARENA_W = 512
VMEM_LIMIT_BYTES = 60 * 1024 * 1024

BF16 = jnp.bfloat16
F32 = jnp.float32

assert D_LRU == D_POOL == FF_CHUNK == ARENA_W
assert N_CHUNKS % CHUNKS_PER_STEP == 0

N_PARAMS = 17
N_STATES = 4


def _rmsnorm(x, g):
    y = x * lax.rsqrt(jnp.mean(x * x, axis=-1, keepdims=True) + EPS)
    return y * g


def _mm(a, b):
    return jnp.dot(a, b, preferred_element_type=F32)


def _no_op():
    pass


def _layer_body(i, x_ref, y_ref, state_in, params, state_out, scratch,
                *, nb, tt, sequential, start, before_y_write=_no_op, after_x_read=_no_op):
    slc_ref, sh_ref, spb_ref, sfb_ref = state_in
    (g1_ref, win_ref, cw_ref, cb_ref, wg_ref, ba_ref, bx_ref, lam_ref,
     wp_ref, ps_ref, wout_ref, g2_ref, up_ref, fw_ref, fb_ref, down_ref, gf_ref) = params
    olc_ref, oh_ref, opb_ref, ofb_ref = state_out
    arena, ab, cx, cp, hst, fcar, hb = scratch
    rows = nb * tt
    lc = (LRU_CONV - 1) * nb
    pc = POOL_BUF * nb
    fc = (FFN_CONV - 1) * nb
    ex0 = 0
    ep0 = ex0 + lc + rows
    hl0 = ep0 + pc + rows
    xc0 = hl0 + rows

    def load_state():
        cx[...] = slc_ref[...]
        cp[...] = spb_ref[...]
        hst[...] = sh_ref[...]
        for k in range(2 * N_CHUNKS):
            fcar[k] = sfb_ref[:, k * FF_CHUNK:(k + 1) * FF_CHUNK]

    if sequential:
        pl.when(i == 0)(load_state)
    else:
        load_state()

    hb[...] = _rmsnorm(x_ref[...], g1_ref[...]).astype(BF16)
    arena[ex0:ex0 + lc] = cx[...]
    arena[ep0:ep0 + pc] = cp[...]
    arena[ex0 + lc:ex0 + lc + rows] = _mm(hb[...], win_ref[:, 0:D_LRU])
    arena[hl0:hl0 + rows] = jax.nn.gelu(_mm(hb[...], win_ref[:, D_LRU:2 * D_LRU]))
    arena[ep0 + pc:ep0 + pc + rows] = _mm(hb[...], win_ref[:, 2 * D_LRU:])
    cx[...] = arena[ex0 + rows:ex0 + rows + lc]
    cp[...] = arena[ep0 + rows:ep0 + rows + pc]

    xc = cb_ref[...] + cw_ref[0:1, :] * arena[ex0:ex0 + rows]
    for j in range(1, LRU_CONV):
        xc = xc + cw_ref[j:j + 1, :] * arena[ex0 + j * nb:ex0 + j * nb + rows]
    arena[xc0:xc0 + rows] = xc

    softplus_neg_lam = jnp.logaddexp(-lam_ref[...], 0.0)
    for k in range(D_LRU // MXU_TILE):
        cs = slice(k * MXU_TILE, (k + 1) * MXU_TILE)
        xck = arena[xc0:xc0 + rows, cs]
        pre = _mm(xck.astype(BF16), wg_ref[k])
        r = jax.nn.sigmoid(pre[:, :MXU_TILE] + ba_ref[:, cs])
        ig = jax.nn.sigmoid(pre[:, MXU_TILE:] + bx_ref[:, cs])
        log_a = -LRU_C * r * softplus_neg_lam[:, cs]
        a = jnp.exp(log_a)
        mult = jnp.sqrt(jnp.tanh(-log_a) * (1.0 + a * a))
        ab[:, k * MXU_TILE:(k + 1) * MXU_TILE] = a
        ab[:, D_LRU + k * MXU_TILE:D_LRU + (k + 1) * MXU_TILE] = mult * (ig * xck)

    def step(t, h):
        r0 = pl.multiple_of(t * nb, nb)
        h = ab[pl.ds(r0, nb), 0:D_LRU] * h + ab[pl.ds(r0, nb), D_LRU:2 * D_LRU]
        arena[pl.ds(hl0 + r0, nb), :] = h * arena[pl.ds(hl0 + r0, nb), :]
        return h

    hst[...] = lax.fori_loop(0, tt, step, hst[...], unroll=min(tt, 8))

    before_y_write()

    t_local = lax.shift_right_logical(lax.broadcasted_iota(jnp.int32, (rows, LANES), 0), nb.bit_length() - 1)
    pos = start + t_local + (i * tt if sequential else 0)
    pooled = []
    for g, w in enumerate(POOL_WINDOWS):
        cs = slice(g * POOL_GROUP_DIM, (g + 1) * POOL_GROUP_DIM)
        cur = arena[ep0 + pc:ep0 + pc + rows, cs]
        s = cur
        for j in range(1, w):
            s = s + arena[ep0 + pc - j * nb:ep0 + pc - j * nb + rows, cs]
        cnt = jnp.minimum(w, pos + 1).astype(F32)
        pooled.append((s / cnt - cur).astype(BF16))

    x1 = x_ref[...] + _mm(arena[hl0:hl0 + rows].astype(BF16), wout_ref[0:D_LRU, :])
    for k in range(D_POOL // MXU_TILE):
        cs = slice(k * MXU_TILE, (k + 1) * MXU_TILE)
        pk = jnp.concatenate(pooled[2 * k:2 * k + 2], axis=-1)
        pool_out = _mm(pk, wp_ref[k]) * ps_ref[:, cs]
        x1 = x1 + _mm(pool_out.astype(BF16), wout_ref[D_LRU + k * MXU_TILE:D_LRU + (k + 1) * MXU_TILE, :])
    y_ref[...] = x1

    hb[...] = _rmsnorm(y_ref[...], g2_ref[...]).astype(BF16)

    after_x_read()

    def ffn_step(p, carry):
        down = None
        for q in range(CHUNKS_PER_STEP):
            c = p * CHUNKS_PER_STEP + q
            branches = []
            for j in range(2):
                base = (2 * q + j) * (fc + rows)
                idx = j * N_CHUNKS + c
                arena[base:base + fc] = fcar[idx]
                arena[base + fc:base + fc + rows] = _mm(hb[...], up_ref[idx])
                fcar[idx] = arena[base + rows:base + rows + fc]
                w = fw_ref[idx]
                u = fb_ref[idx] + w[0:1, :] * arena[base:base + rows]
                for t in range(1, FFN_CONV):
                    u = u + w[t:t + 1, :] * arena[base + t * nb:base + t * nb + rows]
                branches.append(u)
            act = (jax.nn.gelu(branches[0]) * branches[1]).astype(BF16)
            d = _mm(act, down_ref[c])
            down = d if down is None else down + d
        y_ref[...] += down
        return carry

    lax.fori_loop(0, N_CHUNKS // CHUNKS_PER_STEP, ffn_step, 0)

    y_ref[...] = _rmsnorm(y_ref[...], gf_ref[...])

    olc_ref[...] = cx[...]
    opb_ref[...] = cp[...]
    oh_ref[...] = hst[...]
    for k in range(2 * N_CHUNKS):
        ofb_ref[:, k * FF_CHUNK:(k + 1) * FF_CHUNK] = fcar[k]


def _split_refs(refs):
    state_in = refs[:N_STATES]
    params = refs[N_STATES:N_STATES + N_PARAMS]
    return state_in, params, refs[N_STATES + N_PARAMS:]


def _tile_kernel(x_ref, *refs, nb, tt, start):
    state_in, params, rest = _split_refs(refs)
    y_ref, state_out, scratch = rest[0], rest[1:1 + N_STATES], rest[1 + N_STATES:]
    _layer_body(pl.program_id(0), x_ref, y_ref, state_in, params, state_out, scratch,
                nb=nb, tt=tt, sequential=False, start=start)


def _sequence_kernel(x_hbm, *refs, nb, tt, start):
    state_in, params, rest = _split_refs(refs)
    y_hbm, state_out = rest[0], rest[1:1 + N_STATES]
    xbuf, ybuf, sem_x, sem_y = rest[1 + N_STATES:5 + N_STATES]
    scratch = rest[5 + N_STATES:]
    i = pl.program_id(0)
    n = pl.num_programs(0)

    def x_copies(step):
        t0 = pl.multiple_of(step * tt, tt)
        dst = xbuf.reshape(tt, nb, D_MODEL)
        return [pltpu.make_async_copy(x_hbm.at[b, pl.ds(t0, tt), :], dst.at[:, b, :], sem_x.at[b]) for b in range(nb)]

    def y_copies(step):
        t0 = pl.multiple_of(step * tt, tt)
        src = ybuf.reshape(tt, nb, D_MODEL)
        return [pltpu.make_async_copy(src.at[:, b, :], y_hbm.at[b, pl.ds(t0, tt), :], sem_y.at[b]) for b in range(nb)]

    def start_all(copies):
        for c in copies:
            c.start()

    def wait_all(copies):
        for c in copies:
            c.wait()

    pl.when(i == 0)(lambda: start_all(x_copies(0)))
    wait_all(x_copies(i))

    def before_y_write():
        pl.when(i > 0)(lambda: wait_all(y_copies(i - 1)))

    def after_x_read():
        pl.when(i + 1 < n)(lambda: start_all(x_copies(i + 1)))

    _layer_body(i, xbuf, ybuf, state_in, params, state_out, scratch,
                nb=nb, tt=tt, sequential=True, start=start,
                before_y_write=before_y_write, after_x_read=after_x_read)

    start_all(y_copies(i))
    pl.when(i == n - 1)(lambda: wait_all(y_copies(i)))


def _const_spec(shape):
    nd = len(shape)
    return pl.BlockSpec(shape, lambda i: (0,) * nd, pipeline_mode=pl.Buffered(1))


def _state_shapes(nb):
    return [((LRU_CONV - 1) * nb, D_LRU), (nb, D_LRU), (POOL_BUF * nb, D_POOL), ((FFN_CONV - 1) * nb, 2 * D_FF)]


def _work_scratch(nb, tt):
    rows = nb * tt
    lc, pc, fc = (LRU_CONV - 1) * nb, POOL_BUF * nb, (FFN_CONV - 1) * nb
    arena_rows = max(4 * rows + lc + pc, 2 * CHUNKS_PER_STEP * (fc + rows))
    return [
        pltpu.VMEM((arena_rows, ARENA_W), F32),
        pltpu.VMEM((rows, 2 * D_LRU), F32),
        pltpu.VMEM((lc, D_LRU), F32),
        pltpu.VMEM((pc, D_POOL), F32),
        pltpu.VMEM((nb, D_LRU), F32),
        pltpu.VMEM((2 * N_CHUNKS, fc, FF_CHUNK), F32),
        pltpu.VMEM((rows, D_MODEL), BF16),
    ]


def _run_tiles(x_tm, states, params, *, nb, tt, n_tiles, start, name):
    rows = nb * tt
    assert nb % SUBLANES == 0 and nb & (nb - 1) == 0
    tile_map = lambda i: (i, 0)
    state_shapes = _state_shapes(nb)
    in_specs = [pl.BlockSpec((rows, D_MODEL), tile_map)]
    in_specs += [pl.BlockSpec(s, tile_map) for s in state_shapes]
    in_specs += [_const_spec(p.shape) for p in params]
    out_specs = [pl.BlockSpec((rows, D_MODEL), tile_map)]
    out_specs += [pl.BlockSpec(s, tile_map) for s in state_shapes]
    out_shape = [jax.ShapeDtypeStruct(x_tm.shape, F32)]
    out_shape += [jax.ShapeDtypeStruct((n_tiles * s[0], s[1]), F32) for s in state_shapes]
    return pl.pallas_call(
        functools.partial(_tile_kernel, nb=nb, tt=tt, start=start),
        grid=(n_tiles,),
        in_specs=in_specs,
        out_specs=out_specs,
        out_shape=out_shape,
        scratch_shapes=_work_scratch(nb, tt),
        compiler_params=pltpu.CompilerParams(
            dimension_semantics=("arbitrary",), vmem_limit_bytes=VMEM_LIMIT_BYTES),
        name=name,
    )(x_tm, *states, *params)


def _run_sequences(x, states, params, *, tt, start, name):
    nb, s, _ = x.shape
    rows = nb * tt
    assert nb % SUBLANES == 0 and nb & (nb - 1) == 0 and s % tt == 0
    state_map = lambda i: (0, 0)
    state_shapes = _state_shapes(nb)
    in_specs = [pl.BlockSpec(memory_space=pl.ANY)]
    in_specs += [pl.BlockSpec(sh, state_map) for sh in state_shapes]
    in_specs += [_const_spec(p.shape) for p in params]
    out_specs = [pl.BlockSpec(memory_space=pl.ANY)]
    out_specs += [pl.BlockSpec(sh, state_map) for sh in state_shapes]
    out_shape = [jax.ShapeDtypeStruct(x.shape, F32)]
    out_shape += [jax.ShapeDtypeStruct(sh, F32) for sh in state_shapes]
    io_scratch = [
        pltpu.VMEM((rows, D_MODEL), F32),
        pltpu.VMEM((rows, D_MODEL), F32),
        pltpu.SemaphoreType.DMA((nb,)),
        pltpu.SemaphoreType.DMA((nb,)),
    ]
    return pl.pallas_call(
        functools.partial(_sequence_kernel, nb=nb, tt=tt, start=start),
        grid=(s // tt,),
        in_specs=in_specs,
        out_specs=out_specs,
        out_shape=out_shape,
        scratch_shapes=io_scratch + _work_scratch(nb, tt),
        compiler_params=pltpu.CompilerParams(
            dimension_semantics=("arbitrary",), vmem_limit_bytes=VMEM_LIMIT_BYTES),
        name=name,
    )(x, *states, *params)


def _block_diag_tiles(w, per_tile):
    n, d, _ = w.shape
    tiles = []
    for k in range(n // per_tile):
        tiles.append(jax.scipy.linalg.block_diag(*[w[k * per_tile + j] for j in range(per_tile)]))
    return jnp.stack(tiles)


def _chunk_columns(w):
    k = w.shape[0]
    return w.reshape(k, 2 * N_CHUNKS, FF_CHUNK).transpose(1, 0, 2)


def _to_time_major(a, groups):
    b, k, c = a.shape
    return a.reshape(groups, b // groups, k, c).transpose(0, 2, 1, 3).reshape(b * k, c)


def _from_time_major(a, groups, b, k):
    c = a.shape[-1]
    return a.reshape(groups, k, b // groups, c).transpose(0, 2, 1, 3).reshape(b, k, c)


PROMPT_TT = 64
SAMPLE_GROUPS = 4


def kernel(x_prompt, x_sample, state_lru_conv, state_lru_h, state_pool, state_ffn_conv, norm1_g, w_in, lru_conv_w, lru_conv_b, lru_wa, lru_ba, lru_wx, lru_bx, lru_lambda, pool_w, pool_scale, w_out, norm2_g, ffn_up, ffn_conv_w, ffn_conv_b, ffn_down, final_g):
    depth = w_in.shape[0]
    assert depth == 1
    l = 0
    row = lambda v: v.reshape(1, -1)
    per_tile = MXU_TILE // LRU_HEAD_DIM
    wa_t = _block_diag_tiles(lru_wa[l], per_tile)
    wx_t = _block_diag_tiles(lru_wx[l], per_tile)
    params = (
        row(norm1_g[l]), w_in[l].astype(BF16), lru_conv_w[l], row(lru_conv_b[l]),
        jnp.concatenate([wa_t, wx_t], axis=-1).astype(BF16),
        row(lru_ba[l]), row(lru_bx[l]), row(lru_lambda[l]),
        _block_diag_tiles(pool_w[l], MXU_TILE // POOL_GROUP_DIM).astype(BF16), row(pool_scale[l]),
        w_out[l].astype(BF16), row(norm2_g[l]), _chunk_columns(ffn_up[l].astype(BF16)),
        _chunk_columns(ffn_conv_w[l]), _chunk_columns(row(ffn_conv_b[l])),
        ffn_down[l].astype(BF16).reshape(N_CHUNKS, FF_CHUNK, D_MODEL), row(final_g),
    )
    assert len(params) == N_PARAMS

    bp, sp, _ = x_prompt.shape
    p_states = tuple(jnp.zeros(sh, F32) for sh in _state_shapes(bp))
    yp, p_lc, p_h, p_pb, p_fb = _run_sequences(
        x_prompt, p_states, params, tt=PROMPT_TT, start=0, name="layer_prompt")

    bs, ss, _ = x_sample.shape
    g = SAMPLE_GROUPS
    s_states = (_to_time_major(state_lru_conv[l], g), state_lru_h[l],
                _to_time_major(state_pool[l], g), _to_time_major(state_ffn_conv[l], g))
    ys, s_lc, s_h, s_pb, s_fb = _run_tiles(
        _to_time_major(x_sample, g), s_states, params,
        nb=bs // g, tt=ss, n_tiles=g, start=PAST_LEN, name="layer_sample")

    out = (
        yp, _from_time_major(ys, g, bs, ss),
        _from_time_major(p_lc, 1, bp, LRU_CONV - 1)[None], p_h[None],
        _from_time_major(p_pb, 1, bp, POOL_BUF)[None], _from_time_major(p_fb, 1, bp, FFN_CONV - 1)[None],
        _from_time_major(s_lc, g, bs, LRU_CONV - 1)[None], s_h[None],
        _from_time_major(s_pb, g, bs, POOL_BUF)[None], _from_time_major(s_fb, g, bs, FFN_CONV - 1)[None],
    )
    return out
```

```python
import functools

import jax
import jax.numpy as jnp
from jax import lax
from jax.experimental import pallas as pl
from jax.experimental.pallas import tpu as pltpu

D_MODEL = 1024
D_LRU = 512
D_POOL = 512
N_LRU_HEADS = 8
LRU_HEAD_DIM = D_LRU // N_LRU_HEADS
LRU_CONV = 4
LRU_C = 8.0
POOL_WINDOWS = (2, 4, 8, 16)
POOL_GROUP_DIM = D_POOL // len(POOL_WINDOWS)
POOL_BUF = max(POOL_WINDOWS) - 1
D_FF = 3 * D_MODEL
FFN_CONV = 3
EPS = 1e-6
PAST_LEN = 16384

MXU_TILE = 256
LANES = 128
SUBLANES = 8
FF_CHUNK = 512
N_CHUNKS = D_FF // FF_CHUNK
CHUNKS_PER_STEP = 2
ARENA_W = 512
VMEM_LIMIT_BYTES = 60 * 1024 * 1024

BF16 = jnp.bfloat16
F32 = jnp.float32

assert D_LRU == D_POOL == FF_CHUNK == ARENA_W
assert N_CHUNKS % CHUNKS_PER_STEP == 0

N_PARAMS = 17
N_STATES = 4


def _rmsnorm(x, g):
    y = x * lax.rsqrt(jnp.mean(x * x, axis=-1, keepdims=True) + EPS)
    return y * g


def _mm(a, b):
    return jnp.dot(a, b, preferred_element_type=F32)


def _no_op():
    pass


def _layer_body(i, x_ref, y_ref, state_in, params, state_out, scratch,
                *, nb, tt, sequential, start, before_y_write=_no_op, after_x_read=_no_op):
    slc_ref, sh_ref, spb_ref, sfb_ref = state_in
    (g1_ref, win_ref, cw_ref, cb_ref, wg_ref, ba_ref, bx_ref, lam_ref,
     wp_ref, ps_ref, wout_ref, g2_ref, up_ref, fw_ref, fb_ref, down_ref, gf_ref) = params
    olc_ref, oh_ref, opb_ref, ofb_ref = state_out
    arena, ab, cx, cp, hst, fcar, hb = scratch
    rows = nb * tt
    lc = (LRU_CONV - 1) * nb
    pc = POOL_BUF * nb
    fc = (FFN_CONV - 1) * nb
    ex0 = 0
    ep0 = ex0 + lc + rows
    hl0 = ep0 + pc + rows
    xc0 = hl0 + rows

    def load_state():
        cx[...] = slc_ref[...]
        cp[...] = spb_ref[...]
        hst[...] = sh_ref[...]
        for k in range(2 * N_CHUNKS):
            fcar[k] = sfb_ref[:, k * FF_CHUNK:(k + 1) * FF_CHUNK]

    if sequential:
        pl.when(i == 0)(load_state)
    else:
        load_state()

    hb[...] = _rmsnorm(x_ref[...], g1_ref[...]).astype(BF16)
    arena[ex0:ex0 + lc] = cx[...]
    arena[ep0:ep0 + pc] = cp[...]
    arena[ex0 + lc:ex0 + lc + rows] = _mm(hb[...], win_ref[:, 0:D_LRU])
    arena[hl0:hl0 + rows] = jax.nn.gelu(_mm(hb[...], win_ref[:, D_LRU:2 * D_LRU]))
    arena[ep0 + pc:ep0 + pc + rows] = _mm(hb[...], win_ref[:, 2 * D_LRU:])
    cx[...] = arena[ex0 + rows:ex0 + rows + lc]
    cp[...] = arena[ep0 + rows:ep0 + rows + pc]

    xc = cb_ref[...] + cw_ref[0:1, :] * arena[ex0:ex0 + rows]
    for j in range(1, LRU_CONV):
        xc = xc + cw_ref[j:j + 1, :] * arena[ex0 + j * nb:ex0 + j * nb + rows]
    arena[xc0:xc0 + rows] = xc

    softplus_neg_lam = jnp.logaddexp(-lam_ref[...], 0.0)
    for k in range(D_LRU // MXU_TILE):
        cs = slice(k * MXU_TILE, (k + 1) * MXU_TILE)
        xck = arena[xc0:xc0 + rows, cs]
        pre = _mm(xck.astype(BF16), wg_ref[k])
        r = jax.nn.sigmoid(pre[:, :MXU_TILE] + ba_ref[:, cs])
        ig = jax.nn.sigmoid(pre[:, MXU_TILE:] + bx_ref[:, cs])
        log_a = -LRU_C * r * softplus_neg_lam[:, cs]
        a = jnp.exp(log_a)
        mult = jnp.sqrt(jnp.tanh(-log_a) * (1.0 + a * a))
        ab[:, k * MXU_TILE:(k + 1) * MXU_TILE] = a
        ab[:, D_LRU + k * MXU_TILE:D_LRU + (k + 1) * MXU_TILE] = mult * (ig * xck)

    def step(t, h):
        r0 = pl.multiple_of(t * nb, nb)
        h = ab[pl.ds(r0, nb), 0:D_LRU] * h + ab[pl.ds(r0, nb), D_LRU:2 * D_LRU]
        arena[pl.ds(hl0 + r0, nb), :] = h * arena[pl.ds(hl0 + r0, nb), :]
        return h

    hst[...] = lax.fori_loop(0, tt, step, hst[...], unroll=min(tt, 8))

    before_y_write()

    t_local = lax.shift_right_logical(lax.broadcasted_iota(jnp.int32, (rows, LANES), 0), nb.bit_length() - 1)
    pos = start + t_local + (i * tt if sequential else 0)
    pooled = []
    for g, w in enumerate(POOL_WINDOWS):
        cs = slice(g * POOL_GROUP_DIM, (g + 1) * POOL_GROUP_DIM)
        cur = arena[ep0 + pc:ep0 + pc + rows, cs]
        s = cur
        for j in range(1, w):
            s = s + arena[ep0 + pc - j * nb:ep0 + pc - j * nb + rows, cs]
        cnt = jnp.minimum(w, pos + 1).astype(F32)
        pooled.append((s / cnt - cur).astype(BF16))

    x1 = x_ref[...] + _mm(arena[hl0:hl0 + rows].astype(BF16), wout_ref[0:D_LRU, :])
    for k in range(D_POOL // MXU_TILE):
        cs = slice(k * MXU_TILE, (k + 1) * MXU_TILE)
        pk = jnp.concatenate(pooled[2 * k:2 * k + 2], axis=-1)
        pool_out = _mm(pk, wp_ref[k]) * ps_ref[:, cs]
        x1 = x1 + _mm(pool_out.astype(BF16), wout_ref[D_LRU + k * MXU_TILE:D_LRU + (k + 1) * MXU_TILE, :])
    y_ref[...] = x1

    hb[...] = _rmsnorm(y_ref[...], g2_ref[...]).astype(BF16)

    after_x_read()

    def ffn_step(p, carry):
        down = None
        for q in range(CHUNKS_PER_STEP):
            c = p * CHUNKS_PER_STEP + q
            branches = []
            for j in range(2):
                base = (2 * q + j) * (fc + rows)
                idx = j * N_CHUNKS + c
                arena[base:base + fc] = fcar[idx]
                arena[base + fc:base + fc + rows] = _mm(hb[...], up_ref[idx])
                fcar[idx] = arena[base + rows:base + rows + fc]
                w = fw_ref[idx]
                u = fb_ref[idx] + w[0:1, :] * arena[base:base + rows]
                for t in range(1, FFN_CONV):
                    u = u + w[t:t + 1, :] * arena[base + t * nb:base + t * nb + rows]
                branches.append(u)
            act = (jax.nn.gelu(branches[0]) * branches[1]).astype(BF16)
            d = _mm(act, down_ref[c])
            down = d if down is None else down + d
        y_ref[...] += down
        return carry

    lax.fori_loop(0, N_CHUNKS // CHUNKS_PER_STEP, ffn_step, 0)

    y_ref[...] = _rmsnorm(y_ref[...], gf_ref[...])

    olc_ref[...] = cx[...]
    opb_ref[...] = cp[...]
    oh_ref[...] = hst[...]
    for k in range(2 * N_CHUNKS):
        ofb_ref[:, k * FF_CHUNK:(k + 1) * FF_CHUNK] = fcar[k]


def _split_refs(refs):
    state_in = refs[:N_STATES]
    params = refs[N_STATES:N_STATES + N_PARAMS]
    return state_in, params, refs[N_STATES + N_PARAMS:]


def _tile_kernel(x_ref, *refs, nb, tt, start):
    state_in, params, rest = _split_refs(refs)
    y_ref, state_out, scratch = rest[0], rest[1:1 + N_STATES], rest[1 + N_STATES:]
    _layer_body(pl.program_id(0), x_ref, y_ref, state_in, params, state_out, scratch,
                nb=nb, tt=tt, sequential=False, start=start)


def _sequence_kernel(x_hbm, *refs, nb, tt, start):
    state_in, params, rest = _split_refs(refs)
    y_hbm, state_out = rest[0], rest[1:1 + N_STATES]
    xbuf, ybuf, sem_x, sem_y = rest[1 + N_STATES:5 + N_STATES]
    scratch = rest[5 + N_STATES:]
    i = pl.program_id(0)
    n = pl.num_programs(0)

    def x_copies(step):
        t0 = pl.multiple_of(step * tt, tt)
        dst = xbuf.reshape(tt, nb, D_MODEL)
        return [pltpu.make_async_copy(x_hbm.at[b, pl.ds(t0, tt), :], dst.at[:, b, :], sem_x.at[b]) for b in range(nb)]

    def y_copies(step):
        t0 = pl.multiple_of(step * tt, tt)
        src = ybuf.reshape(tt, nb, D_MODEL)
        return [pltpu.make_async_copy(src.at[:, b, :], y_hbm.at[b, pl.ds(t0, tt), :], sem_y.at[b]) for b in range(nb)]

    def start_all(copies):
        for c in copies:
            c.start()

    def wait_all(copies):
        for c in copies:
            c.wait()

    pl.when(i == 0)(lambda: start_all(x_copies(0)))
    wait_all(x_copies(i))

    def before_y_write():
        pl.when(i > 0)(lambda: wait_all(y_copies(i - 1)))

    def after_x_read():
        pl.when(i + 1 < n)(lambda: start_all(x_copies(i + 1)))

    _layer_body(i, xbuf, ybuf, state_in, params, state_out, scratch,
                nb=nb, tt=tt, sequential=True, start=start,
                before_y_write=before_y_write, after_x_read=after_x_read)

    start_all(y_copies(i))
    pl.when(i == n - 1)(lambda: wait_all(y_copies(i)))


def _const_spec(shape):
    nd = len(shape)
    return pl.BlockSpec(shape, lambda i: (0,) * nd, pipeline_mode=pl.Buffered(1))


def _state_shapes(nb):
    return [((LRU_CONV - 1) * nb, D_LRU), (nb, D_LRU), (POOL_BUF * nb, D_POOL), ((FFN_CONV - 1) * nb, 2 * D_FF)]


def _work_scratch(nb, tt):
    rows = nb * tt
    lc, pc, fc = (LRU_CONV - 1) * nb, POOL_BUF * nb, (FFN_CONV - 1) * nb
    arena_rows = max(4 * rows + lc + pc, 2 * CHUNKS_PER_STEP * (fc + rows))
    return [
        pltpu.VMEM((arena_rows, ARENA_W), F32),
        pltpu.VMEM((rows, 2 * D_LRU), F32),
        pltpu.VMEM((lc, D_LRU), F32),
        pltpu.VMEM((pc, D_POOL), F32),
        pltpu.VMEM((nb, D_LRU), F32),
        pltpu.VMEM((2 * N_CHUNKS, fc, FF_CHUNK), F32),
        pltpu.VMEM((rows, D_MODEL), BF16),
    ]


def _run_tiles(x_tm, states, params, *, nb, tt, n_tiles, start, name):
    rows = nb * tt
    assert nb % SUBLANES == 0 and nb & (nb - 1) == 0
    tile_map = lambda i: (i, 0)
    state_shapes = _state_shapes(nb)
    in_specs = [pl.BlockSpec((rows, D_MODEL), tile_map)]
    in_specs += [pl.BlockSpec(s, tile_map) for s in state_shapes]
    in_specs += [_const_spec(p.shape) for p in params]
    out_specs = [pl.BlockSpec((rows, D_MODEL), tile_map)]
    out_specs += [pl.BlockSpec(s, tile_map) for s in state_shapes]
    out_shape = [jax.ShapeDtypeStruct(x_tm.shape, F32)]
    out_shape += [jax.ShapeDtypeStruct((n_tiles * s[0], s[1]), F32) for s in state_shapes]
    return pl.pallas_call(
        functools.partial(_tile_kernel, nb=nb, tt=tt, start=start),
        grid=(n_tiles,),
        in_specs=in_specs,
        out_specs=out_specs,
        out_shape=out_shape,
        scratch_shapes=_work_scratch(nb, tt),
        compiler_params=pltpu.CompilerParams(
            dimension_semantics=("arbitrary",), vmem_limit_bytes=VMEM_LIMIT_BYTES),
        name=name,
    )(x_tm, *states, *params)


def _run_sequences(x, states, params, *, tt, start, name):
    nb, s, _ = x.shape
    rows = nb * tt
    assert nb % SUBLANES == 0 and nb & (nb - 1) == 0 and s % tt == 0
    state_map = lambda i: (0, 0)
    state_shapes = _state_shapes(nb)
    in_specs = [pl.BlockSpec(memory_space=pl.ANY)]
    in_specs += [pl.BlockSpec(sh, state_map) for sh in state_shapes]
    in_specs += [_const_spec(p.shape) for p in params]
    out_specs = [pl.BlockSpec(memory_space=pl.ANY)]
    out_specs += [pl.BlockSpec(sh, state_map) for sh in state_shapes]
    out_shape = [jax.ShapeDtypeStruct(x.shape, F32)]
    out_shape += [jax.ShapeDtypeStruct(sh, F32) for sh in state_shapes]
    io_scratch = [
        pltpu.VMEM((rows, D_MODEL), F32),
        pltpu.VMEM((rows, D_MODEL), F32),
        pltpu.SemaphoreType.DMA((nb,)),
        pltpu.SemaphoreType.DMA((nb,)),
    ]
    return pl.pallas_call(
        functools.partial(_sequence_kernel, nb=nb, tt=tt, start=start),
        grid=(s // tt,),
        in_specs=in_specs,
        out_specs=out_specs,
        out_shape=out_shape,
        scratch_shapes=io_scratch + _work_scratch(nb, tt),
        compiler_params=pltpu.CompilerParams(
            dimension_semantics=("arbitrary",), vmem_limit_bytes=VMEM_LIMIT_BYTES),
        name=name,
    )(x, *states, *params)


def _block_diag_tiles(w, per_tile):
    n, d, _ = w.shape
    tiles = []
    for k in range(n // per_tile):
        tiles.append(jax.scipy.linalg.block_diag(*[w[k * per_tile + j] for j in range(per_tile)]))
    return jnp.stack(tiles)


def _chunk_columns(w):
    k = w.shape[0]
    return w.reshape(k, 2 * N_CHUNKS, FF_CHUNK).transpose(1, 0, 2)


def _to_time_major(a, groups):
    b, k, c = a.shape
    return a.reshape(groups, b // groups, k, c).transpose(0, 2, 1, 3).reshape(b * k, c)


def _from_time_major(a, groups, b, k):
    c = a.shape[-1]
    return a.reshape(groups, k, b // groups, c).transpose(0, 2, 1, 3).reshape(b, k, c)


PROMPT_TT = 128
SAMPLE_GROUPS = 4


def kernel(x_prompt, x_sample, state_lru_conv, state_lru_h, state_pool, state_ffn_conv, norm1_g, w_in, lru_conv_w, lru_conv_b, lru_wa, lru_ba, lru_wx, lru_bx, lru_lambda, pool_w, pool_scale, w_out, norm2_g, ffn_up, ffn_conv_w, ffn_conv_b, ffn_down, final_g):
    depth = w_in.shape[0]
    assert depth == 1
    l = 0
    row = lambda v: v.reshape(1, -1)
    per_tile = MXU_TILE // LRU_HEAD_DIM
    wa_t = _block_diag_tiles(lru_wa[l], per_tile)
    wx_t = _block_diag_tiles(lru_wx[l], per_tile)
    params = (
        row(norm1_g[l]), w_in[l].astype(BF16), lru_conv_w[l], row(lru_conv_b[l]),
        jnp.concatenate([wa_t, wx_t], axis=-1).astype(BF16),
        row(lru_ba[l]), row(lru_bx[l]), row(lru_lambda[l]),
        _block_diag_tiles(pool_w[l], MXU_TILE // POOL_GROUP_DIM).astype(BF16), row(pool_scale[l]),
        w_out[l].astype(BF16), row(norm2_g[l]), _chunk_columns(ffn_up[l]).astype(BF16),
        _chunk_columns(ffn_conv_w[l]), _chunk_columns(row(ffn_conv_b[l])),
        ffn_down[l].astype(BF16).reshape(N_CHUNKS, FF_CHUNK, D_MODEL), row(final_g),
    )
    assert len(params) == N_PARAMS

    bp, sp, _ = x_prompt.shape
    p_states = tuple(jnp.zeros(sh, F32) for sh in _state_shapes(bp))
    yp, p_lc, p_h, p_pb, p_fb = _run_sequences(
        x_prompt, p_states, params, tt=PROMPT_TT, start=0, name="layer_prompt")

    bs, ss, _ = x_sample.shape
    g = SAMPLE_GROUPS
    s_states = (_to_time_major(state_lru_conv[l], g), state_lru_h[l],
                _to_time_major(state_pool[l], g), _to_time_major(state_ffn_conv[l], g))
    ys, s_lc, s_h, s_pb, s_fb = _run_tiles(
        _to_time_major(x_sample, g), s_states, params,
        nb=bs // g, tt=ss, n_tiles=g, start=PAST_LEN, name="layer_sample")

    out = (
        yp, _from_time_major(ys, g, bs, ss),
        _from_time_major(p_lc, 1, bp, LRU_CONV - 1)[None], p_h[None],
        _from_time_major(p_pb, 1, bp, POOL_BUF)[None], _from_time_major(p_fb, 1, bp, FFN_CONV - 1)[None],
        _from_time_major(s_lc, g, bs, LRU_CONV - 1)[None], s_h[None],
        _from_time_major(s_pb, g, bs, POOL_BUF)[None], _from_time_major(s_fb, g, bs, FFN_CONV - 1)[None],
    )
    return out
```

```python
import functools

import jax
import jax.numpy as jnp
from jax import lax
from jax.experimental import pallas as pl
from jax.experimental.pallas import tpu as pltpu

D_MODEL = 1024
D_LRU = 512
D_POOL = 512
N_LRU_HEADS = 8
LRU_HEAD_DIM = D_LRU // N_LRU_HEADS
LRU_CONV = 4
LRU_C = 8.0
POOL_WINDOWS = (2, 4, 8, 16)
POOL_GROUP_DIM = D_POOL // len(POOL_WINDOWS)
POOL_BUF = max(POOL_WINDOWS) - 1
D_FF = 3 * D_MODEL
FFN_CONV = 3
EPS = 1e-6
PAST_LEN = 16384

MXU_TILE = 256
LANES = 128
SUBLANES = 8
FF_CHUNK = 512
N_CHUNKS = D_FF // FF_CHUNK
CHUNKS_PER_STEP = 2
ARENA_W = 512
VMEM_LIMIT_BYTES = 60 * 1024 * 1024

BF16 = jnp.bfloat16
F32 = jnp.float32

assert D_LRU == D_POOL == FF_CHUNK == ARENA_W
assert N_CHUNKS % CHUNKS_PER_STEP == 0

N_PARAMS = 17
N_STATES = 4


def _rmsnorm(x, g):
    y = x * lax.rsqrt(jnp.mean(x * x, axis=-1, keepdims=True) + EPS)
    return y * g


def _mm(a, b):
    return jnp.dot(a, b, preferred_element_type=F32)


def _no_op():
    pass


def _layer_body(i, x_ref, y_ref, state_in, params, state_out, scratch,
                *, nb, tt, sequential, start, before_y_write=_no_op, after_x_read=_no_op):
    slc_ref, sh_ref, spb_ref, sfb_ref = state_in
    (g1_ref, win_ref, cw_ref, cb_ref, wg_ref, ba_ref, bx_ref, lam_ref,
     wp_ref, ps_ref, wout_ref, g2_ref, up_ref, fw_ref, fb_ref, down_ref, gf_ref) = params
    olc_ref, oh_ref, opb_ref, ofb_ref = state_out
    arena, ab, cx, cp, hst, fcar, hb = scratch
    rows = nb * tt
    lc = (LRU_CONV - 1) * nb
    pc = POOL_BUF * nb
    fc = (FFN_CONV - 1) * nb
    ex0 = 0
    ep0 = ex0 + lc + rows
    hl0 = ep0 + pc + rows
    xc0 = hl0 + rows

    def load_state():
        cx[...] = slc_ref[...]
        cp[...] = spb_ref[...]
        hst[...] = sh_ref[...]
        for k in range(2 * N_CHUNKS):
            fcar[k] = sfb_ref[:, k * FF_CHUNK:(k + 1) * FF_CHUNK]

    if sequential:
        pl.when(i == 0)(load_state)
    else:
        load_state()

    hb[...] = _rmsnorm(x_ref[...], g1_ref[...]).astype(BF16)
    arena[ex0:ex0 + lc] = cx[...]
    arena[ep0:ep0 + pc] = cp[...]
    arena[ex0 + lc:ex0 + lc + rows] = _mm(hb[...], win_ref[:, 0:D_LRU])
    arena[hl0:hl0 + rows] = jax.nn.gelu(_mm(hb[...], win_ref[:, D_LRU:2 * D_LRU]))
    arena[ep0 + pc:ep0 + pc + rows] = _mm(hb[...], win_ref[:, 2 * D_LRU:])
    cx[...] = arena[ex0 + rows:ex0 + rows + lc]
    cp[...] = arena[ep0 + rows:ep0 + rows + pc]

    xc = cb_ref[...] + cw_ref[0:1, :] * arena[ex0:ex0 + rows]
    for j in range(1, LRU_CONV):
        xc = xc + cw_ref[j:j + 1, :] * arena[ex0 + j * nb:ex0 + j * nb + rows]
    arena[xc0:xc0 + rows] = xc

    softplus_neg_lam = jnp.logaddexp(-lam_ref[...], 0.0)
    for k in range(D_LRU // MXU_TILE):
        cs = slice(k * MXU_TILE, (k + 1) * MXU_TILE)
        xck = arena[xc0:xc0 + rows, cs]
        pre = _mm(xck.astype(BF16), wg_ref[k])
        r = jax.nn.sigmoid(pre[:, :MXU_TILE] + ba_ref[:, cs])
        ig = jax.nn.sigmoid(pre[:, MXU_TILE:] + bx_ref[:, cs])
        log_a = -LRU_C * r * softplus_neg_lam[:, cs]
        a = jnp.exp(log_a)
        mult = jnp.sqrt(jnp.tanh(-log_a) * (1.0 + a * a))
        ab[:, k * MXU_TILE:(k + 1) * MXU_TILE] = a
        ab[:, D_LRU + k * MXU_TILE:D_LRU + (k + 1) * MXU_TILE] = mult * (ig * xck)

    h = hst[...]
    for t in range(tt):
        r0 = t * nb
        h = ab[r0:r0 + nb, 0:D_LRU] * h + ab[r0:r0 + nb, D_LRU:2 * D_LRU]
        arena[hl0 + r0:hl0 + r0 + nb, :] = h * arena[hl0 + r0:hl0 + r0 + nb, :]
    hst[...] = h

    t_local = lax.shift_right_logical(lax.broadcasted_iota(jnp.int32, (rows, LANES), 0), nb.bit_length() - 1)
    pos = start + t_local + (i * tt if sequential else 0)
    pooled = []
    for g, w in enumerate(POOL_WINDOWS):
        cs = slice(g * POOL_GROUP_DIM, (g + 1) * POOL_GROUP_DIM)
        cur = arena[ep0 + pc:ep0 + pc + rows, cs]
        s = cur
        for j in range(1, w):
            s = s + arena[ep0 + pc - j * nb:ep0 + pc - j * nb + rows, cs]
        cnt = jnp.minimum(w, pos + 1).astype(F32)
        pooled.append((s / cnt - cur).astype(BF16))

    x1 = x_ref[...] + _mm(arena[hl0:hl0 + rows].astype(BF16), wout_ref[0:D_LRU, :])
    for k in range(D_POOL // MXU_TILE):
        cs = slice(k * MXU_TILE, (k + 1) * MXU_TILE)
        pk = jnp.concatenate(pooled[2 * k:2 * k + 2], axis=-1)
        pool_out = _mm(pk, wp_ref[k]) * ps_ref[:, cs]
        x1 = x1 + _mm(pool_out.astype(BF16), wout_ref[D_LRU + k * MXU_TILE:D_LRU + (k + 1) * MXU_TILE, :])
    ab[...] = x1

    hb[...] = _rmsnorm(ab[...], g2_ref[...]).astype(BF16)

    after_x_read()

    def ffn_step(p, carry):
        down = None
        for q in range(CHUNKS_PER_STEP):
            c = p * CHUNKS_PER_STEP + q
            branches = []
            for j in range(2):
                base = (2 * q + j) * (fc + rows)
                idx = j * N_CHUNKS + c
                arena[base:base + fc] = fcar[idx]
                arena[base + fc:base + fc + rows] = _mm(hb[...], up_ref[idx])
                fcar[idx] = arena[base + rows:base + rows + fc]
                w = fw_ref[idx]
                u = fb_ref[idx] + w[0:1, :] * arena[base:base + rows]
                for t in range(1, FFN_CONV):
                    u = u + w[t:t + 1, :] * arena[base + t * nb:base + t * nb + rows]
                branches.append(u)
            act = (jax.nn.gelu(branches[0]) * branches[1]).astype(BF16)
            d = _mm(act, down_ref[c])
            down = d if down is None else down + d
        ab[...] += down
        return carry

    lax.fori_loop(0, N_CHUNKS // CHUNKS_PER_STEP, ffn_step, 0)

    before_y_write()
    y_ref[...] = _rmsnorm(ab[...], gf_ref[...])

    olc_ref[...] = cx[...]
    opb_ref[...] = cp[...]
    oh_ref[...] = hst[...]
    for k in range(2 * N_CHUNKS):
        ofb_ref[:, k * FF_CHUNK:(k + 1) * FF_CHUNK] = fcar[k]


def _split_refs(refs):
    state_in = refs[:N_STATES]
    params = refs[N_STATES:N_STATES + N_PARAMS]
    return state_in, params, refs[N_STATES + N_PARAMS:]


def _tile_kernel(x_ref, *refs, nb, tt, start):
    state_in, params, rest = _split_refs(refs)
    y_ref, state_out, scratch = rest[0], rest[1:1 + N_STATES], rest[1 + N_STATES:]
    _layer_body(pl.program_id(0), x_ref, y_ref, state_in, params, state_out, scratch,
                nb=nb, tt=tt, sequential=False, start=start)


def _sequence_kernel(x_hbm, *refs, nb, tt, start):
    state_in, params, rest = _split_refs(refs)
    y_hbm, state_out = rest[0], rest[1:1 + N_STATES]
    xbuf, ybuf, sem_x, sem_y = rest[1 + N_STATES:5 + N_STATES]
    scratch = rest[5 + N_STATES:]
    i = pl.program_id(0)
    n = pl.num_programs(0)

    def x_copies(step):
        t0 = pl.multiple_of(step * tt, tt)
        dst = xbuf.reshape(tt, nb, D_MODEL)
        return [pltpu.make_async_copy(x_hbm.at[b, pl.ds(t0, tt), :], dst.at[:, b, :], sem_x.at[b]) for b in range(nb)]

    def y_copies(step):
        t0 = pl.multiple_of(step * tt, tt)
        src = ybuf.reshape(tt, nb, D_MODEL)
        return [pltpu.make_async_copy(src.at[:, b, :], y_hbm.at[b, pl.ds(t0, tt), :], sem_y.at[b]) for b in range(nb)]

    def start_all(copies):
        for c in copies:
            c.start()

    def wait_all(copies):
        for c in copies:
            c.wait()

    pl.when(i == 0)(lambda: start_all(x_copies(0)))
    wait_all(x_copies(i))

    def before_y_write():
        pl.when(i > 0)(lambda: wait_all(y_copies(i - 1)))

    def after_x_read():
        pl.when(i + 1 < n)(lambda: start_all(x_copies(i + 1)))

    _layer_body(i, xbuf, ybuf, state_in, params, state_out, scratch,
                nb=nb, tt=tt, sequential=True, start=start,
                before_y_write=before_y_write, after_x_read=after_x_read)

    start_all(y_copies(i))
    pl.when(i == n - 1)(lambda: wait_all(y_copies(i)))


def _const_spec(shape):
    nd = len(shape)
    return pl.BlockSpec(shape, lambda i: (0,) * nd, pipeline_mode=pl.Buffered(1))


def _state_shapes(nb):
    return [((LRU_CONV - 1) * nb, D_LRU), (nb, D_LRU), (POOL_BUF * nb, D_POOL), ((FFN_CONV - 1) * nb, 2 * D_FF)]


def _work_scratch(nb, tt):
    rows = nb * tt
    lc, pc, fc = (LRU_CONV - 1) * nb, POOL_BUF * nb, (FFN_CONV - 1) * nb
    arena_rows = max(4 * rows + lc + pc, 2 * CHUNKS_PER_STEP * (fc + rows))
    return [
        pltpu.VMEM((arena_rows, ARENA_W), F32),
        pltpu.VMEM((rows, D_MODEL), F32),
        pltpu.VMEM((lc, D_LRU), F32),
        pltpu.VMEM((pc, D_POOL), F32),
        pltpu.VMEM((nb, D_LRU), F32),
        pltpu.VMEM((2 * N_CHUNKS, fc, FF_CHUNK), F32),
        pltpu.VMEM((rows, D_MODEL), BF16),
    ]


def _run_tiles(x_tm, states, params, *, nb, tt, n_tiles, start, name):
    rows = nb * tt
    assert nb % SUBLANES == 0 and nb & (nb - 1) == 0
    tile_map = lambda i: (i, 0)
    state_shapes = _state_shapes(nb)
    in_specs = [pl.BlockSpec((rows, D_MODEL), tile_map)]
    in_specs += [pl.BlockSpec(s, tile_map) for s in state_shapes]
    in_specs += [_const_spec(p.shape) for p in params]
    out_specs = [pl.BlockSpec((rows, D_MODEL), tile_map)]
    out_specs += [pl.BlockSpec(s, tile_map) for s in state_shapes]
    out_shape = [jax.ShapeDtypeStruct(x_tm.shape, F32)]
    out_shape += [jax.ShapeDtypeStruct((n_tiles * s[0], s[1]), F32) for s in state_shapes]
    return pl.pallas_call(
        functools.partial(_tile_kernel, nb=nb, tt=tt, start=start),
        grid=(n_tiles,),
        in_specs=in_specs,
        out_specs=out_specs,
        out_shape=out_shape,
        scratch_shapes=_work_scratch(nb, tt),
        compiler_params=pltpu.CompilerParams(
            dimension_semantics=("arbitrary",), vmem_limit_bytes=VMEM_LIMIT_BYTES),
        name=name,
    )(x_tm, *states, *params)


def _run_sequences(x, states, params, *, tt, start, name):
    nb, s, _ = x.shape
    rows = nb * tt
    assert nb % SUBLANES == 0 and nb & (nb - 1) == 0 and s % tt == 0
    state_map = lambda i: (0, 0)
    state_shapes = _state_shapes(nb)
    in_specs = [pl.BlockSpec(memory_space=pl.ANY)]
    in_specs += [pl.BlockSpec(sh, state_map) for sh in state_shapes]
    in_specs += [_const_spec(p.shape) for p in params]
    out_specs = [pl.BlockSpec(memory_space=pl.ANY)]
    out_specs += [pl.BlockSpec(sh, state_map) for sh in state_shapes]
    out_shape = [jax.ShapeDtypeStruct(x.shape, F32)]
    out_shape += [jax.ShapeDtypeStruct(sh, F32) for sh in state_shapes]
    io_scratch = [
        pltpu.VMEM((rows, D_MODEL), F32),
        pltpu.VMEM((rows, D_MODEL), F32),
        pltpu.SemaphoreType.DMA((nb,)),
        pltpu.SemaphoreType.DMA((nb,)),
    ]
    return pl.pallas_call(
        functools.partial(_sequence_kernel, nb=nb, tt=tt, start=start),
        grid=(s // tt,),
        in_specs=in_specs,
        out_specs=out_specs,
        out_shape=out_shape,
        scratch_shapes=io_scratch + _work_scratch(nb, tt),
        compiler_params=pltpu.CompilerParams(
            dimension_semantics=("arbitrary",), vmem_limit_bytes=VMEM_LIMIT_BYTES),
        name=name,
    )(x, *states, *params)


def _block_diag_tiles(w, per_tile):
    n, d, _ = w.shape
    tiles = []
    for k in range(n // per_tile):
        tiles.append(jax.scipy.linalg.block_diag(*[w[k * per_tile + j] for j in range(per_tile)]))
    return jnp.stack(tiles)


def _chunk_columns(w):
    k = w.shape[0]
    return w.reshape(k, 2 * N_CHUNKS, FF_CHUNK).transpose(1, 0, 2)


def _to_time_major(a, groups):
    b, k, c = a.shape
    return a.reshape(groups, b // groups, k, c).transpose(0, 2, 1, 3).reshape(b * k, c)


def _from_time_major(a, groups, b, k):
    c = a.shape[-1]
    return a.reshape(groups, k, b // groups, c).transpose(0, 2, 1, 3).reshape(b, k, c)


PROMPT_TT = 128
SAMPLE_GROUPS = 4


def kernel(x_prompt, x_sample, state_lru_conv, state_lru_h, state_pool, state_ffn_conv, norm1_g, w_in, lru_conv_w, lru_conv_b, lru_wa, lru_ba, lru_wx, lru_bx, lru_lambda, pool_w, pool_scale, w_out, norm2_g, ffn_up, ffn_conv_w, ffn_conv_b, ffn_down, final_g):
    depth = w_in.shape[0]
    assert depth == 1
    l = 0
    row = lambda v: v.reshape(1, -1)
    per_tile = MXU_TILE // LRU_HEAD_DIM
    wa_t = _block_diag_tiles(lru_wa[l], per_tile)
    wx_t = _block_diag_tiles(lru_wx[l], per_tile)
    params = (
        row(norm1_g[l]), w_in[l].astype(BF16), lru_conv_w[l], row(lru_conv_b[l]),
        jnp.concatenate([wa_t, wx_t], axis=-1).astype(BF16),
        row(lru_ba[l]), row(lru_bx[l]), row(lru_lambda[l]),
        _block_diag_tiles(pool_w[l], MXU_TILE // POOL_GROUP_DIM).astype(BF16), row(pool_scale[l]),
        w_out[l].astype(BF16), row(norm2_g[l]), _chunk_columns(ffn_up[l]).astype(BF16),
        _chunk_columns(ffn_conv_w[l]), _chunk_columns(row(ffn_conv_b[l])),
        ffn_down[l].astype(BF16).reshape(N_CHUNKS, FF_CHUNK, D_MODEL), row(final_g),
    )
    assert len(params) == N_PARAMS

    bp, sp, _ = x_prompt.shape
    p_states = tuple(jnp.zeros(sh, F32) for sh in _state_shapes(bp))
    yp, p_lc, p_h, p_pb, p_fb = _run_sequences(
        x_prompt, p_states, params, tt=PROMPT_TT, start=0, name="layer_prompt")

    bs, ss, _ = x_sample.shape
    g = SAMPLE_GROUPS
    s_states = (_to_time_major(state_lru_conv[l], g), state_lru_h[l],
                _to_time_major(state_pool[l], g), _to_time_major(state_ffn_conv[l], g))
    ys, s_lc, s_h, s_pb, s_fb = _run_tiles(
        _to_time_major(x_sample, g), s_states, params,
        nb=bs // g, tt=ss, n_tiles=g, start=PAST_LEN, name="layer_sample")

    out = (
        yp, _from_time_major(ys, g, bs, ss),
        _from_time_major(p_lc, 1, bp, LRU_CONV - 1)[None], p_h[None],
        _from_time_major(p_pb, 1, bp, POOL_BUF)[None], _from_time_major(p_fb, 1, bp, FFN_CONV - 1)[None],
        _from_time_major(s_lc, g, bs, LRU_CONV - 1)[None], s_h[None],
        _from_time_major(s_pb, g, bs, POOL_BUF)[None], _from_time_major(s_fb, g, bs, FFN_CONV - 1)[None],
    )
    return out
```

```python
import functools

import jax
import jax.numpy as jnp
from jax import lax
from jax.experimental import pallas as pl
from jax.experimental.pallas import tpu as pltpu

D_MODEL = 1024
D_LRU = 512
D_POOL = 512
N_LRU_HEADS = 8
LRU_HEAD_DIM = D_LRU // N_LRU_HEADS
LRU_CONV = 4
LRU_C = 8.0
POOL_WINDOWS = (2, 4, 8, 16)
POOL_GROUP_DIM = D_POOL // len(POOL_WINDOWS)
POOL_BUF = max(POOL_WINDOWS) - 1
D_FF = 3 * D_MODEL
FFN_CONV = 3
EPS = 1e-6
PAST_LEN = 16384

MXU_TILE = 256
LANES = 128
SUBLANES = 8
FF_CHUNK = 512
N_CHUNKS = D_FF // FF_CHUNK
CHUNKS_PER_STEP = 2
ARENA_W = 512
VMEM_LIMIT_BYTES = 60 * 1024 * 1024

BF16 = jnp.bfloat16
F32 = jnp.float32

assert D_LRU == D_POOL == FF_CHUNK == ARENA_W
assert N_CHUNKS % CHUNKS_PER_STEP == 0

N_PARAMS = 17
UP_PARAM = 12
N_STATES = 4


def _rmsnorm(x, g):
    y = x * lax.rsqrt(jnp.mean(x * x, axis=-1, keepdims=True) + EPS)
    return y * g


def _mm(a, b):
    return jnp.dot(a, b, preferred_element_type=F32)


def _no_op():
    pass


def _layer_body(i, x_ref, y_ref, state_in, params, state_out, scratch,
                *, nb, tt, sequential, start, before_y_write=_no_op, after_x_read=_no_op):
    slc_ref, sh_ref, spb_ref, sfb_ref = state_in
    (g1_ref, win_ref, cw_ref, cb_ref, wg_ref, ba_ref, bx_ref, lam_ref,
     wp_ref, ps_ref, wout_ref, g2_ref, up_hbm, fw_ref, fb_ref, down_ref, gf_ref) = params
    olc_ref, oh_ref, opb_ref, ofb_ref = state_out
    arena, ab, cx, cp, hst, fcar, hb, up_ref, sem_up = scratch

    def up_copies():
        return [pltpu.make_async_copy(up_hbm.at[:, k * FF_CHUNK:(k + 1) * FF_CHUNK], up_ref.at[k], sem_up.at[k])
                for k in range(2 * N_CHUNKS)]

    def start_up():
        for c in up_copies():
            c.start()

    def wait_up():
        for c in up_copies():
            c.wait()

    pl.when(i == 0)(start_up)
    rows = nb * tt
    lc = (LRU_CONV - 1) * nb
    pc = POOL_BUF * nb
    fc = (FFN_CONV - 1) * nb
    ex0 = 0
    ep0 = ex0 + lc + rows
    hl0 = ep0 + pc + rows
    xc0 = hl0 + rows

    def load_state():
        cx[...] = slc_ref[...]
        cp[...] = spb_ref[...]
        hst[...] = sh_ref[...]
        for k in range(2 * N_CHUNKS):
            fcar[k] = sfb_ref[:, k * FF_CHUNK:(k + 1) * FF_CHUNK]

    if sequential:
        pl.when(i == 0)(load_state)
    else:
        load_state()

    hb[...] = _rmsnorm(x_ref[...], g1_ref[...]).astype(BF16)
    arena[ex0:ex0 + lc] = cx[...]
    arena[ep0:ep0 + pc] = cp[...]
    arena[ex0 + lc:ex0 + lc + rows] = _mm(hb[...], win_ref[:, 0:D_LRU])
    arena[hl0:hl0 + rows] = jax.nn.gelu(_mm(hb[...], win_ref[:, D_LRU:2 * D_LRU]))
    arena[ep0 + pc:ep0 + pc + rows] = _mm(hb[...], win_ref[:, 2 * D_LRU:])
    cx[...] = arena[ex0 + rows:ex0 + rows + lc]
    cp[...] = arena[ep0 + rows:ep0 + rows + pc]

    xc = cb_ref[...] + cw_ref[0:1, :] * arena[ex0:ex0 + rows]
    for j in range(1, LRU_CONV):
        xc = xc + cw_ref[j:j + 1, :] * arena[ex0 + j * nb:ex0 + j * nb + rows]
    arena[xc0:xc0 + rows] = xc

    softplus_neg_lam = jnp.logaddexp(-lam_ref[...], 0.0)
    for k in range(D_LRU // MXU_TILE):
        cs = slice(k * MXU_TILE, (k + 1) * MXU_TILE)
        xck = arena[xc0:xc0 + rows, cs]
        pre = _mm(xck.astype(BF16), wg_ref[k])
        r = jax.nn.sigmoid(pre[:, :MXU_TILE] + ba_ref[:, cs])
        ig = jax.nn.sigmoid(pre[:, MXU_TILE:] + bx_ref[:, cs])
        log_a = -LRU_C * r * softplus_neg_lam[:, cs]
        a = jnp.exp(log_a)
        mult = jnp.sqrt(jnp.tanh(-log_a) * (1.0 + a * a))
        ab[:, k * MXU_TILE:(k + 1) * MXU_TILE] = a
        ab[:, D_LRU + k * MXU_TILE:D_LRU + (k + 1) * MXU_TILE] = mult * (ig * xck)

    h = hst[...]
    for t in range(tt):
        r0 = t * nb
        h = ab[r0:r0 + nb, 0:D_LRU] * h + ab[r0:r0 + nb, D_LRU:2 * D_LRU]
        arena[hl0 + r0:hl0 + r0 + nb, :] = h * arena[hl0 + r0:hl0 + r0 + nb, :]
    hst[...] = h

    t_local = lax.shift_right_logical(lax.broadcasted_iota(jnp.int32, (rows, LANES), 0), nb.bit_length() - 1)
    pos = start + t_local + (i * tt if sequential else 0)
    pooled = []
    for g, w in enumerate(POOL_WINDOWS):
        cs = slice(g * POOL_GROUP_DIM, (g + 1) * POOL_GROUP_DIM)
        cur = arena[ep0 + pc:ep0 + pc + rows, cs]
        s = cur
        for j in range(1, w):
            s = s + arena[ep0 + pc - j * nb:ep0 + pc - j * nb + rows, cs]
        cnt = jnp.minimum(w, pos + 1).astype(F32)
        pooled.append((s / cnt - cur).astype(BF16))

    x1 = x_ref[...] + _mm(arena[hl0:hl0 + rows].astype(BF16), wout_ref[0:D_LRU, :])
    for k in range(D_POOL // MXU_TILE):
        cs = slice(k * MXU_TILE, (k + 1) * MXU_TILE)
        pk = jnp.concatenate(pooled[2 * k:2 * k + 2], axis=-1)
        pool_out = _mm(pk, wp_ref[k]) * ps_ref[:, cs]
        x1 = x1 + _mm(pool_out.astype(BF16), wout_ref[D_LRU + k * MXU_TILE:D_LRU + (k + 1) * MXU_TILE, :])
    ab[...] = x1

    hb[...] = _rmsnorm(ab[...], g2_ref[...]).astype(BF16)

    after_x_read()
    pl.when(i == 0)(wait_up)

    def ffn_step(p, carry):
        down = None
        for q in range(CHUNKS_PER_STEP):
            c = p * CHUNKS_PER_STEP + q
            branches = []
            for j in range(2):
                base = (2 * q + j) * (fc + rows)
                idx = j * N_CHUNKS + c
                arena[base:base + fc] = fcar[idx]
                arena[base + fc:base + fc + rows] = _mm(hb[...], up_ref[idx])
                fcar[idx] = arena[base + rows:base + rows + fc]
                w = fw_ref[idx]
                u = fb_ref[idx] + w[0:1, :] * arena[base:base + rows]
                for t in range(1, FFN_CONV):
                    u = u + w[t:t + 1, :] * arena[base + t * nb:base + t * nb + rows]
                branches.append(u)
            act = (jax.nn.gelu(branches[0]) * branches[1]).astype(BF16)
            d = _mm(act, down_ref[c])
            down = d if down is None else down + d
        ab[...] += down
        return carry

    lax.fori_loop(0, N_CHUNKS // CHUNKS_PER_STEP, ffn_step, 0)

    before_y_write()
    y_ref[...] = _rmsnorm(ab[...], gf_ref[...])

    olc_ref[...] = cx[...]
    opb_ref[...] = cp[...]
    oh_ref[...] = hst[...]
    for k in range(2 * N_CHUNKS):
        ofb_ref[:, k * FF_CHUNK:(k + 1) * FF_CHUNK] = fcar[k]


def _split_refs(refs):
    state_in = refs[:N_STATES]
    params = refs[N_STATES:N_STATES + N_PARAMS]
    return state_in, params, refs[N_STATES + N_PARAMS:]


def _tile_kernel(x_ref, *refs, nb, tt, start):
    state_in, params, rest = _split_refs(refs)
    y_ref, state_out, scratch = rest[0], rest[1:1 + N_STATES], rest[1 + N_STATES:]
    _layer_body(pl.program_id(0), x_ref, y_ref, state_in, params, state_out, scratch,
                nb=nb, tt=tt, sequential=False, start=start)


def _sequence_kernel(x_hbm, *refs, nb, tt, start):
    state_in, params, rest = _split_refs(refs)
    y_hbm, state_out = rest[0], rest[1:1 + N_STATES]
    xbuf, ybuf, sem_x, sem_y = rest[1 + N_STATES:5 + N_STATES]
    scratch = rest[5 + N_STATES:]
    i = pl.program_id(0)
    n = pl.num_programs(0)

    def x_copies(step):
        t0 = pl.multiple_of(step * tt, tt)
        dst = xbuf.reshape(tt, nb, D_MODEL)
        return [pltpu.make_async_copy(x_hbm.at[b, pl.ds(t0, tt), :], dst.at[:, b, :], sem_x.at[b]) for b in range(nb)]

    def y_copies(step):
        t0 = pl.multiple_of(step * tt, tt)
        src = ybuf.reshape(tt, nb, D_MODEL)
        return [pltpu.make_async_copy(src.at[:, b, :], y_hbm.at[b, pl.ds(t0, tt), :], sem_y.at[b]) for b in range(nb)]

    def start_all(copies):
        for c in copies:
            c.start()

    def wait_all(copies):
        for c in copies:
            c.wait()

    pl.when(i == 0)(lambda: start_all(x_copies(0)))
    wait_all(x_copies(i))

    def before_y_write():
        pl.when(i > 0)(lambda: wait_all(y_copies(i - 1)))

    def after_x_read():
        pl.when(i + 1 < n)(lambda: start_all(x_copies(i + 1)))

    _layer_body(i, xbuf, ybuf, state_in, params, state_out, scratch,
                nb=nb, tt=tt, sequential=True, start=start,
                before_y_write=before_y_write, after_x_read=after_x_read)

    start_all(y_copies(i))
    pl.when(i == n - 1)(lambda: wait_all(y_copies(i)))


def _const_spec(shape):
    nd = len(shape)
    return pl.BlockSpec(shape, lambda i: (0,) * nd, pipeline_mode=pl.Buffered(1))


def _param_specs(params):
    return [pl.BlockSpec(memory_space=pl.ANY) if k == UP_PARAM else _const_spec(p.shape)
            for k, p in enumerate(params)]


def _state_shapes(nb):
    return [((LRU_CONV - 1) * nb, D_LRU), (nb, D_LRU), (POOL_BUF * nb, D_POOL), ((FFN_CONV - 1) * nb, 2 * D_FF)]


def _work_scratch(nb, tt):
    rows = nb * tt
    lc, pc, fc = (LRU_CONV - 1) * nb, POOL_BUF * nb, (FFN_CONV - 1) * nb
    arena_rows = max(4 * rows + lc + pc, 2 * CHUNKS_PER_STEP * (fc + rows))
    return [
        pltpu.VMEM((arena_rows, ARENA_W), F32),
        pltpu.VMEM((rows, D_MODEL), F32),
        pltpu.VMEM((lc, D_LRU), F32),
        pltpu.VMEM((pc, D_POOL), F32),
        pltpu.VMEM((nb, D_LRU), F32),
        pltpu.VMEM((2 * N_CHUNKS, fc, FF_CHUNK), F32),
        pltpu.VMEM((rows, D_MODEL), BF16),
        pltpu.VMEM((2 * N_CHUNKS, D_MODEL, FF_CHUNK), BF16),
        pltpu.SemaphoreType.DMA((2 * N_CHUNKS,)),
    ]


def _run_tiles(x_tm, states, params, *, nb, tt, n_tiles, start, name):
    rows = nb * tt
    assert nb % SUBLANES == 0 and nb & (nb - 1) == 0
    tile_map = lambda i: (i, 0)
    state_shapes = _state_shapes(nb)
    in_specs = [pl.BlockSpec((rows, D_MODEL), tile_map)]
    in_specs += [pl.BlockSpec(s, tile_map) for s in state_shapes]
    in_specs += _param_specs(params)
    out_specs = [pl.BlockSpec((rows, D_MODEL), tile_map)]
    out_specs += [pl.BlockSpec(s, tile_map) for s in state_shapes]
    out_shape = [jax.ShapeDtypeStruct(x_tm.shape, F32)]
    out_shape += [jax.ShapeDtypeStruct((n_tiles * s[0], s[1]), F32) for s in state_shapes]
    return pl.pallas_call(
        functools.partial(_tile_kernel, nb=nb, tt=tt, start=start),
        grid=(n_tiles,),
        in_specs=in_specs,
        out_specs=out_specs,
        out_shape=out_shape,
        scratch_shapes=_work_scratch(nb, tt),
        compiler_params=pltpu.CompilerParams(
            dimension_semantics=("arbitrary",), vmem_limit_bytes=VMEM_LIMIT_BYTES),
        name=name,
    )(x_tm, *states, *params)


def _run_sequences(x, states, params, *, tt, start, name):
    nb, s, _ = x.shape
    rows = nb * tt
    assert nb % SUBLANES == 0 and nb & (nb - 1) == 0 and s % tt == 0
    state_map = lambda i: (0, 0)
    state_shapes = _state_shapes(nb)
    in_specs = [pl.BlockSpec(memory_space=pl.ANY)]
    in_specs += [pl.BlockSpec(sh, state_map) for sh in state_shapes]
    in_specs += _param_specs(params)
    out_specs = [pl.BlockSpec(memory_space=pl.ANY)]
    out_specs += [pl.BlockSpec(sh, state_map) for sh in state_shapes]
    out_shape = [jax.ShapeDtypeStruct(x.shape, F32)]
    out_shape += [jax.ShapeDtypeStruct(sh, F32) for sh in state_shapes]
    io_scratch = [
        pltpu.VMEM((rows, D_MODEL), F32),
        pltpu.VMEM((rows, D_MODEL), F32),
        pltpu.SemaphoreType.DMA((nb,)),
        pltpu.SemaphoreType.DMA((nb,)),
    ]
    return pl.pallas_call(
        functools.partial(_sequence_kernel, nb=nb, tt=tt, start=start),
        grid=(s // tt,),
        in_specs=in_specs,
        out_specs=out_specs,
        out_shape=out_shape,
        scratch_shapes=io_scratch + _work_scratch(nb, tt),
        compiler_params=pltpu.CompilerParams(
            dimension_semantics=("arbitrary",), vmem_limit_bytes=VMEM_LIMIT_BYTES),
        name=name,
    )(x, *states, *params)


def _block_diag_tiles(w, per_tile):
    n, d, _ = w.shape
    tiles = []
    for k in range(n // per_tile):
        tiles.append(jax.scipy.linalg.block_diag(*[w[k * per_tile + j] for j in range(per_tile)]))
    return jnp.stack(tiles)


def _chunk_columns(w):
    k = w.shape[0]
    return w.reshape(k, 2 * N_CHUNKS, FF_CHUNK).transpose(1, 0, 2)


def _to_time_major(a, groups):
    b, k, c = a.shape
    return a.reshape(groups, b // groups, k, c).transpose(0, 2, 1, 3).reshape(b * k, c)


def _from_time_major(a, groups, b, k):
    c = a.shape[-1]
    return a.reshape(groups, k, b // groups, c).transpose(0, 2, 1, 3).reshape(b, k, c)


PROMPT_TT = 128
SAMPLE_GROUPS = 4


def kernel(x_prompt, x_sample, state_lru_conv, state_lru_h, state_pool, state_ffn_conv, norm1_g, w_in, lru_conv_w, lru_conv_b, lru_wa, lru_ba, lru_wx, lru_bx, lru_lambda, pool_w, pool_scale, w_out, norm2_g, ffn_up, ffn_conv_w, ffn_conv_b, ffn_down, final_g):
    depth = w_in.shape[0]
    assert depth == 1
    l = 0
    row = lambda v: v.reshape(1, -1)
    per_tile = MXU_TILE // LRU_HEAD_DIM
    wa_t = _block_diag_tiles(lru_wa[l], per_tile)
    wx_t = _block_diag_tiles(lru_wx[l], per_tile)
    params = (
        row(norm1_g[l]), w_in[l].astype(BF16), lru_conv_w[l], row(lru_conv_b[l]),
        jnp.concatenate([wa_t, wx_t], axis=-1).astype(BF16),
        row(lru_ba[l]), row(lru_bx[l]), row(lru_lambda[l]),
        _block_diag_tiles(pool_w[l], MXU_TILE // POOL_GROUP_DIM).astype(BF16), row(pool_scale[l]),
        w_out[l].astype(BF16), row(norm2_g[l]), ffn_up[l].astype(BF16),
        _chunk_columns(ffn_conv_w[l]), _chunk_columns(row(ffn_conv_b[l])),
        ffn_down[l].astype(BF16).reshape(N_CHUNKS, FF_CHUNK, D_MODEL), row(final_g),
    )
    assert len(params) == N_PARAMS

    bp, sp, _ = x_prompt.shape
    p_states = tuple(jnp.zeros(sh, F32) for sh in _state_shapes(bp))
    yp, p_lc, p_h, p_pb, p_fb = _run_sequences(
        x_prompt, p_states, params, tt=PROMPT_TT, start=0, name="layer_prompt")

    bs, ss, _ = x_sample.shape
    g = SAMPLE_GROUPS
    s_states = (_to_time_major(state_lru_conv[l], g), state_lru_h[l],
                _to_time_major(state_pool[l], g), _to_time_major(state_ffn_conv[l], g))
    ys, s_lc, s_h, s_pb, s_fb = _run_tiles(
        _to_time_major(x_sample, g), s_states, params,
        nb=bs // g, tt=ss, n_tiles=g, start=PAST_LEN, name="layer_sample")

    out = (
        yp, _from_time_major(ys, g, bs, ss),
        _from_time_major(p_lc, 1, bp, LRU_CONV - 1)[None], p_h[None],
        _from_time_major(p_pb, 1, bp, POOL_BUF)[None], _from_time_major(p_fb, 1, bp, FFN_CONV - 1)[None],
        _from_time_major(s_lc, g, bs, LRU_CONV - 1)[None], s_h[None],
        _from_time_major(s_pb, g, bs, POOL_BUF)[None], _from_time_major(s_fb, g, bs, FFN_CONV - 1)[None],
    )
    return out
```

```python
import functools

import jax
import jax.numpy as jnp
from jax import lax
from jax.experimental import pallas as pl
from jax.experimental.pallas import tpu as pltpu

D_MODEL = 1024
D_LRU = 512
D_POOL = 512
N_LRU_HEADS = 8
LRU_HEAD_DIM = D_LRU // N_LRU_HEADS
LRU_CONV = 4
LRU_C = 8.0
POOL_WINDOWS = (2, 4, 8, 16)
POOL_GROUP_DIM = D_POOL // len(POOL_WINDOWS)
POOL_BUF = max(POOL_WINDOWS) - 1
D_FF = 3 * D_MODEL
FFN_CONV = 3
EPS = 1e-6
PAST_LEN = 16384

MXU_TILE = 256
LANES = 128
SUBLANES = 8
FF_CHUNK = 512
N_CHUNKS = D_FF // FF_CHUNK
CHUNKS_PER_STEP = 2
ARENA_W = 512
VMEM_LIMIT_BYTES = 60 * 1024 * 1024

BF16 = jnp.bfloat16
F32 = jnp.float32

assert D_LRU == D_POOL == FF_CHUNK == ARENA_W
assert N_CHUNKS % CHUNKS_PER_STEP == 0

N_PARAMS = 17
UP_PARAM = 12
N_STATES = 4


def _rmsnorm(x, g):
    y = x * lax.rsqrt(jnp.mean(x * x, axis=-1, keepdims=True) + EPS)
    return y * g


def _mm(a, b):
    return jnp.dot(a, b, preferred_element_type=F32)


def _no_op():
    pass


def _layer_body(i, x_ref, y_ref, state_in, params, state_out, scratch,
                *, nb, tt, sequential, start,
                after_state_read=_no_op, after_x_read=_no_op, before_y_write=_no_op):
    slc_ref, sh_ref, spb_ref, sfb_ref = state_in
    (g1_ref, win_ref, cw_ref, cb_ref, wg_ref, ba_ref, bx_ref, lam_ref,
     wp_ref, ps_ref, wout_ref, g2_ref, up_hbm, fw_ref, fb_ref, down_ref, gf_ref) = params
    olc_ref, oh_ref, opb_ref, ofb_ref = state_out
    arena, ab, cx, cp, hst, fcar, hb, up_ref, sem_up = scratch

    def up_copies():
        return [pltpu.make_async_copy(up_hbm.at[:, k * FF_CHUNK:(k + 1) * FF_CHUNK], up_ref.at[k], sem_up.at[k])
                for k in range(2 * N_CHUNKS)]

    def start_up():
        for c in up_copies():
            c.start()

    def wait_up():
        for c in up_copies():
            c.wait()

    pl.when(i == 0)(start_up)
    rows = nb * tt
    lc = (LRU_CONV - 1) * nb
    pc = POOL_BUF * nb
    fc = (FFN_CONV - 1) * nb
    ex0 = 0
    ep0 = ex0 + lc + rows
    hl0 = ep0 + pc + rows
    xc0 = hl0 + rows

    def load_state():
        cx[...] = slc_ref[...]
        cp[...] = spb_ref[...]
        hst[...] = sh_ref[...]
        for k in range(2 * N_CHUNKS):
            fcar[k] = sfb_ref[:, k * FF_CHUNK:(k + 1) * FF_CHUNK]

    if sequential:
        pl.when(i == 0)(load_state)
    else:
        load_state()
    after_state_read()

    hb[...] = _rmsnorm(x_ref[...], g1_ref[...]).astype(BF16)
    arena[ex0:ex0 + lc] = cx[...]
    arena[ep0:ep0 + pc] = cp[...]
    arena[ex0 + lc:ex0 + lc + rows] = _mm(hb[...], win_ref[:, 0:D_LRU])
    arena[hl0:hl0 + rows] = jax.nn.gelu(_mm(hb[...], win_ref[:, D_LRU:2 * D_LRU]))
    arena[ep0 + pc:ep0 + pc + rows] = _mm(hb[...], win_ref[:, 2 * D_LRU:])
    cx[...] = arena[ex0 + rows:ex0 + rows + lc]
    cp[...] = arena[ep0 + rows:ep0 + rows + pc]

    xc = cb_ref[...] + cw_ref[0:1, :] * arena[ex0:ex0 + rows]
    for j in range(1, LRU_CONV):
        xc = xc + cw_ref[j:j + 1, :] * arena[ex0 + j * nb:ex0 + j * nb + rows]
    arena[xc0:xc0 + rows] = xc

    softplus_neg_lam = jnp.logaddexp(-lam_ref[...], 0.0)
    for k in range(D_LRU // MXU_TILE):
        cs = slice(k * MXU_TILE, (k + 1) * MXU_TILE)
        xck = arena[xc0:xc0 + rows, cs]
        pre = _mm(xck.astype(BF16), wg_ref[k])
        r = jax.nn.sigmoid(pre[:, :MXU_TILE] + ba_ref[:, cs])
        ig = jax.nn.sigmoid(pre[:, MXU_TILE:] + bx_ref[:, cs])
        log_a = -LRU_C * r * softplus_neg_lam[:, cs]
        a = jnp.exp(log_a)
        mult = jnp.sqrt(jnp.tanh(-log_a) * (1.0 + a * a))
        ab[:, k * MXU_TILE:(k + 1) * MXU_TILE] = a
        ab[:, D_LRU + k * MXU_TILE:D_LRU + (k + 1) * MXU_TILE] = mult * (ig * xck)

    h = hst[...]
    for t in range(tt):
        r0 = t * nb
        h = ab[r0:r0 + nb, 0:D_LRU] * h + ab[r0:r0 + nb, D_LRU:2 * D_LRU]
        arena[hl0 + r0:hl0 + r0 + nb, :] = h * arena[hl0 + r0:hl0 + r0 + nb, :]
    hst[...] = h

    t_local = lax.shift_right_logical(lax.broadcasted_iota(jnp.int32, (rows, LANES), 0), nb.bit_length() - 1)
    pos = start + t_local + (i * tt if sequential else 0)
    pooled = []
    for g, w in enumerate(POOL_WINDOWS):
        cs = slice(g * POOL_GROUP_DIM, (g + 1) * POOL_GROUP_DIM)
        cur = arena[ep0 + pc:ep0 + pc + rows, cs]
        s = cur
        for j in range(1, w):
            s = s + arena[ep0 + pc - j * nb:ep0 + pc - j * nb + rows, cs]
        cnt = jnp.minimum(w, pos + 1).astype(F32)
        pooled.append((s / cnt - cur).astype(BF16))

    x1 = x_ref[...] + _mm(arena[hl0:hl0 + rows].astype(BF16), wout_ref[0:D_LRU, :])
    for k in range(D_POOL // MXU_TILE):
        cs = slice(k * MXU_TILE, (k + 1) * MXU_TILE)
        pk = jnp.concatenate(pooled[2 * k:2 * k + 2], axis=-1)
        pool_out = _mm(pk, wp_ref[k]) * ps_ref[:, cs]
        x1 = x1 + _mm(pool_out.astype(BF16), wout_ref[D_LRU + k * MXU_TILE:D_LRU + (k + 1) * MXU_TILE, :])
    ab[...] = x1

    hb[...] = _rmsnorm(ab[...], g2_ref[...]).astype(BF16)

    after_x_read()
    pl.when(i == 0)(wait_up)

    def ffn_step(p, carry):
        down = None
        for q in range(CHUNKS_PER_STEP):
            c = p * CHUNKS_PER_STEP + q
            branches = []
            for j in range(2):
                base = (2 * q + j) * (fc + rows)
                idx = j * N_CHUNKS + c
                arena[base:base + fc] = fcar[idx]
                arena[base + fc:base + fc + rows] = _mm(hb[...], up_ref[idx])
                fcar[idx] = arena[base + rows:base + rows + fc]
                w = fw_ref[idx]
                u = fb_ref[idx] + w[0:1, :] * arena[base:base + rows]
                for t in range(1, FFN_CONV):
                    u = u + w[t:t + 1, :] * arena[base + t * nb:base + t * nb + rows]
                branches.append(u)
            act = (jax.nn.gelu(branches[0]) * branches[1]).astype(BF16)
            d = _mm(act, down_ref[c])
            down = d if down is None else down + d
        ab[...] += down
        return carry

    lax.fori_loop(0, N_CHUNKS // CHUNKS_PER_STEP, ffn_step, 0)

    before_y_write()
    y_ref[...] = _rmsnorm(ab[...], gf_ref[...])

    olc_ref[...] = cx[...]
    opb_ref[...] = cp[...]
    oh_ref[...] = hst[...]
    for k in range(2 * N_CHUNKS):
        ofb_ref[:, k * FF_CHUNK:(k + 1) * FF_CHUNK] = fcar[k]


def _split_refs(refs):
    state_in = refs[:N_STATES]
    params = refs[N_STATES:N_STATES + N_PARAMS]
    return state_in, params, refs[N_STATES + N_PARAMS:]


def _group_kernel(x_hbm, *refs, nb, tt, start):
    state_hbm, params, rest = _split_refs(refs)
    y_hbm, state_out_hbm = rest[0], rest[1:1 + N_STATES]
    n_io = 1 + N_STATES
    stage_in = rest[1 + N_STATES:1 + N_STATES + n_io]
    stage_out = rest[1 + N_STATES + n_io:1 + N_STATES + 2 * n_io]
    sem_in, sem_out = rest[1 + N_STATES + 2 * n_io:3 + N_STATES + 2 * n_io]
    scratch = rest[3 + N_STATES + 2 * n_io:]
    g = pl.program_id(0)
    n = pl.num_programs(0)

    def copies(hbm, vmem, group, to_vmem, sem, batch_major=False):
        b0 = pl.multiple_of(group * nb, nb)
        if len(hbm.shape) == 2:
            pairs = [(hbm.at[pl.ds(b0, nb), :], vmem)]
        elif batch_major:
            pairs = [(hbm.at[pl.ds(b0, nb), k, :], vmem.at[pl.ds(k * nb, nb), :]) for k in range(hbm.shape[1])]
        else:
            pairs = [(hbm.at[k, pl.ds(b0, nb), :], vmem.at[pl.ds(k * nb, nb), :]) for k in range(hbm.shape[0])]
        return [pltpu.make_async_copy(h, v, sem) if to_vmem else pltpu.make_async_copy(v, h, sem) for h, v in pairs]

    def x_copies(group):
        return copies(x_hbm, stage_in[0], group, True, sem_in.at[0], batch_major=True)

    def state_copies(group):
        return [c for j in range(N_STATES)
                for c in copies(state_hbm[j], stage_in[1 + j], group, True, sem_in.at[1 + j])]

    def out_copies(group):
        cs = copies(y_hbm, stage_out[0], group, False, sem_out.at[0], batch_major=True)
        for j in range(N_STATES):
            cs += copies(state_out_hbm[j], stage_out[1 + j], group, False, sem_out.at[1 + j])
        return cs

    def start_all(cs):
        for c in cs:
            c.start()

    def wait_all(cs):
        for c in cs:
            c.wait()

    pl.when(g == 0)(lambda: start_all(state_copies(0) + x_copies(0)))
    wait_all(state_copies(g) + x_copies(g))

    def after_state_read():
        pl.when(g + 1 < n)(lambda: start_all(state_copies(g + 1)))

    def after_x_read():
        pl.when(g + 1 < n)(lambda: start_all(x_copies(g + 1)))

    def before_y_write():
        pl.when(g > 0)(lambda: wait_all(out_copies(g - 1)))

    _layer_body(g, stage_in[0], stage_out[0], stage_in[1:], params, stage_out[1:], scratch,
                nb=nb, tt=tt, sequential=False, start=start,
                after_state_read=after_state_read, after_x_read=after_x_read, before_y_write=before_y_write)

    start_all(out_copies(g))
    pl.when(g == n - 1)(lambda: wait_all(out_copies(g)))


def _sequence_kernel(x_hbm, *refs, nb, tt, start):
    state_in, params, rest = _split_refs(refs)
    y_hbm, state_out = rest[0], rest[1:1 + N_STATES]
    xbuf, ybuf, sem_x, sem_y = rest[1 + N_STATES:5 + N_STATES]
    scratch = rest[5 + N_STATES:]
    i = pl.program_id(0)
    n = pl.num_programs(0)

    def x_copies(step):
        t0 = pl.multiple_of(step * tt, tt)
        dst = xbuf.reshape(tt, nb, D_MODEL)
        return [pltpu.make_async_copy(x_hbm.at[b, pl.ds(t0, tt), :], dst.at[:, b, :], sem_x.at[b]) for b in range(nb)]

    def y_copies(step):
        t0 = pl.multiple_of(step * tt, tt)
        src = ybuf.reshape(tt, nb, D_MODEL)
        return [pltpu.make_async_copy(src.at[:, b, :], y_hbm.at[b, pl.ds(t0, tt), :], sem_y.at[b]) for b in range(nb)]

    def start_all(copies):
        for c in copies:
            c.start()

    def wait_all(copies):
        for c in copies:
            c.wait()

    pl.when(i == 0)(lambda: start_all(x_copies(0)))
    wait_all(x_copies(i))

    def before_y_write():
        pl.when(i > 0)(lambda: wait_all(y_copies(i - 1)))

    def after_x_read():
        pl.when(i + 1 < n)(lambda: start_all(x_copies(i + 1)))

    _layer_body(i, xbuf, ybuf, state_in, params, state_out, scratch,
                nb=nb, tt=tt, sequential=True, start=start,
                before_y_write=before_y_write, after_x_read=after_x_read)

    start_all(y_copies(i))
    pl.when(i == n - 1)(lambda: wait_all(y_copies(i)))


def _const_spec(shape):
    nd = len(shape)
    return pl.BlockSpec(shape, lambda i: (0,) * nd, pipeline_mode=pl.Buffered(1))


def _param_specs(params):
    return [pl.BlockSpec(memory_space=pl.ANY) if k == UP_PARAM else _const_spec(p.shape)
            for k, p in enumerate(params)]


def _state_shapes(nb):
    return [((LRU_CONV - 1) * nb, D_LRU), (nb, D_LRU), (POOL_BUF * nb, D_POOL), ((FFN_CONV - 1) * nb, 2 * D_FF)]


def _work_scratch(nb, tt):
    rows = nb * tt
    lc, pc, fc = (LRU_CONV - 1) * nb, POOL_BUF * nb, (FFN_CONV - 1) * nb
    arena_rows = max(4 * rows + lc + pc, 2 * CHUNKS_PER_STEP * (fc + rows))
    return [
        pltpu.VMEM((arena_rows, ARENA_W), F32),
        pltpu.VMEM((rows, D_MODEL), F32),
        pltpu.VMEM((lc, D_LRU), F32),
        pltpu.VMEM((pc, D_POOL), F32),
        pltpu.VMEM((nb, D_LRU), F32),
        pltpu.VMEM((2 * N_CHUNKS, fc, FF_CHUNK), F32),
        pltpu.VMEM((rows, D_MODEL), BF16),
        pltpu.VMEM((2 * N_CHUNKS, D_MODEL, FF_CHUNK), BF16),
        pltpu.SemaphoreType.DMA((2 * N_CHUNKS,)),
    ]


def _run_groups(x, states, params, *, groups, start, name):
    b, tt, _ = x.shape
    nb = b // groups
    assert nb % SUBLANES == 0 and nb & (nb - 1) == 0 and nb * groups == b and tt % SUBLANES == 0
    arrays = (x,) + tuple(states)
    any_spec = pl.BlockSpec(memory_space=pl.ANY)
    stage_shapes = [(nb * tt, D_MODEL)] + _state_shapes(nb)
    io_scratch = [pltpu.VMEM(s, F32) for s in stage_shapes]
    io_scratch += [pltpu.VMEM(s, F32) for s in stage_shapes]
    io_scratch += [pltpu.SemaphoreType.DMA((len(arrays),)), pltpu.SemaphoreType.DMA((len(arrays),))]
    return pl.pallas_call(
        functools.partial(_group_kernel, nb=nb, tt=tt, start=start),
        grid=(groups,),
        in_specs=[any_spec] * len(arrays) + _param_specs(params),
        out_specs=[any_spec] * len(arrays),
        out_shape=[jax.ShapeDtypeStruct(a.shape, F32) for a in arrays],
        scratch_shapes=io_scratch + _work_scratch(nb, tt),
        compiler_params=pltpu.CompilerParams(
            dimension_semantics=("arbitrary",), vmem_limit_bytes=VMEM_LIMIT_BYTES),
        name=name,
    )(*arrays, *params)


def _run_sequences(x, states, params, *, tt, start, name):
    nb, s, _ = x.shape
    rows = nb * tt
    assert nb % SUBLANES == 0 and nb & (nb - 1) == 0 and s % tt == 0
    state_map = lambda i: (0, 0)
    state_shapes = _state_shapes(nb)
    in_specs = [pl.BlockSpec(memory_space=pl.ANY)]
    in_specs += [pl.BlockSpec(sh, state_map) for sh in state_shapes]
    in_specs += _param_specs(params)
    out_specs = [pl.BlockSpec(memory_space=pl.ANY)]
    out_specs += [pl.BlockSpec(sh, state_map) for sh in state_shapes]
    out_shape = [jax.ShapeDtypeStruct(x.shape, F32)]
    out_shape += [jax.ShapeDtypeStruct(sh, F32) for sh in state_shapes]
    io_scratch = [
        pltpu.VMEM((rows, D_MODEL), F32),
        pltpu.VMEM((rows, D_MODEL), F32),
        pltpu.SemaphoreType.DMA((nb,)),
        pltpu.SemaphoreType.DMA((nb,)),
    ]
    return pl.pallas_call(
        functools.partial(_sequence_kernel, nb=nb, tt=tt, start=start),
        grid=(s // tt,),
        in_specs=in_specs,
        out_specs=out_specs,
        out_shape=out_shape,
        scratch_shapes=io_scratch + _work_scratch(nb, tt),
        compiler_params=pltpu.CompilerParams(
            dimension_semantics=("arbitrary",), vmem_limit_bytes=VMEM_LIMIT_BYTES),
        name=name,
    )(x, *states, *params)


def _block_diag_tiles(w, per_tile):
    n, d, _ = w.shape
    tiles = []
    for k in range(n // per_tile):
        tiles.append(jax.scipy.linalg.block_diag(*[w[k * per_tile + j] for j in range(per_tile)]))
    return jnp.stack(tiles)


def _chunk_columns(w):
    k = w.shape[0]
    return w.reshape(k, 2 * N_CHUNKS, FF_CHUNK).transpose(1, 0, 2)


def _to_time_major(a, groups):
    b, k, c = a.shape
    return a.reshape(groups, b // groups, k, c).transpose(0, 2, 1, 3).reshape(b * k, c)


def _from_time_major(a, groups, b, k):
    c = a.shape[-1]
    return a.reshape(groups, k, b // groups, c).transpose(0, 2, 1, 3).reshape(b, k, c)


PROMPT_TT = 128
SAMPLE_GROUPS = 2


def kernel(x_prompt, x_sample, state_lru_conv, state_lru_h, state_pool, state_ffn_conv, norm1_g, w_in, lru_conv_w, lru_conv_b, lru_wa, lru_ba, lru_wx, lru_bx, lru_lambda, pool_w, pool_scale, w_out, norm2_g, ffn_up, ffn_conv_w, ffn_conv_b, ffn_down, final_g):
    depth = w_in.shape[0]
    assert depth == 1
    l = 0
    row = lambda v: v.reshape(1, -1)
    per_tile = MXU_TILE // LRU_HEAD_DIM
    wa_t = _block_diag_tiles(lru_wa[l], per_tile)
    wx_t = _block_diag_tiles(lru_wx[l], per_tile)
    params = (
        row(norm1_g[l]), w_in[l].astype(BF16), lru_conv_w[l], row(lru_conv_b[l]),
        jnp.concatenate([wa_t, wx_t], axis=-1).astype(BF16),
        row(lru_ba[l]), row(lru_bx[l]), row(lru_lambda[l]),
        _block_diag_tiles(pool_w[l], MXU_TILE // POOL_GROUP_DIM).astype(BF16), row(pool_scale[l]),
        w_out[l].astype(BF16), row(norm2_g[l]), ffn_up[l].astype(BF16),
        _chunk_columns(ffn_conv_w[l]), _chunk_columns(row(ffn_conv_b[l])),
        ffn_down[l].astype(BF16).reshape(N_CHUNKS, FF_CHUNK, D_MODEL), row(final_g),
    )
    assert len(params) == N_PARAMS

    bp, sp, _ = x_prompt.shape
    p_states = tuple(jnp.zeros(sh, F32) for sh in _state_shapes(bp))
    yp, p_lc, p_h, p_pb, p_fb = _run_sequences(
        x_prompt, p_states, params, tt=PROMPT_TT, start=0, name="layer_prompt")

    step_major = lambda a: a.transpose(1, 0, 2)
    s_states = (step_major(state_lru_conv[l]), state_lru_h[l], step_major(state_pool[l]), step_major(state_ffn_conv[l]))
    ys, s_lc, s_h, s_pb, s_fb = _run_groups(
        x_sample, s_states, params, groups=SAMPLE_GROUPS, start=PAST_LEN, name="layer_sample")

    out = (
        yp, ys,
        _from_time_major(p_lc, 1, bp, LRU_CONV - 1)[None], p_h[None],
        _from_time_major(p_pb, 1, bp, POOL_BUF)[None], _from_time_major(p_fb, 1, bp, FFN_CONV - 1)[None],
        step_major(s_lc)[None], s_h[None], step_major(s_pb)[None], step_major(s_fb)[None],
    )
    return out
```

```python
import functools

import jax
import jax.numpy as jnp
from jax import lax
from jax.experimental import pallas as pl
from jax.experimental.pallas import tpu as pltpu

D_MODEL = 1024
D_LRU = 512
D_POOL = 512
N_LRU_HEADS = 8
LRU_HEAD_DIM = D_LRU // N_LRU_HEADS
LRU_CONV = 4
LRU_C = 8.0
POOL_WINDOWS = (2, 4, 8, 16)
POOL_GROUP_DIM = D_POOL // len(POOL_WINDOWS)
POOL_BUF = max(POOL_WINDOWS) - 1
D_FF = 3 * D_MODEL
FFN_CONV = 3
EPS = 1e-6
PAST_LEN = 16384

MXU_TILE = 256
LANES = 128
SUBLANES = 8
FF_CHUNK = 512
N_CHUNKS = D_FF // FF_CHUNK
CHUNKS_PER_STEP = 2
ARENA_W = 512
VMEM_LIMIT_BYTES = 60 * 1024 * 1024

BF16 = jnp.bfloat16
F32 = jnp.float32

assert D_LRU == D_POOL == FF_CHUNK == ARENA_W
assert N_CHUNKS % CHUNKS_PER_STEP == 0

N_PARAMS = 17
UP_PARAM = 12
N_STATES = 4


def _rmsnorm(x, g):
    y = x * lax.rsqrt(jnp.mean(x * x, axis=-1, keepdims=True) + EPS)
    return y * g


GELU_K0 = 0.7978845608028654
GELU_K1 = GELU_K0 * 0.044715


def _gelu_tanh(x):
    half = 0.5 * x
    return half + half * jnp.tanh(x * (GELU_K0 + GELU_K1 * (x * x)))


def _mm(a, b):
    return jnp.dot(a, b, preferred_element_type=F32)


def _no_op():
    pass


def _layer_body(i, x_ref, y_ref, state_in, params, state_out, scratch,
                *, nb, tt, sequential, start,
                after_state_read=_no_op, after_x_read=_no_op, before_y_write=_no_op):
    slc_ref, sh_ref, spb_ref, sfb_ref = state_in
    (g1_ref, win_ref, cw_ref, cb_ref, wg_ref, ba_ref, bx_ref, lam_ref,
     wp_ref, ps_ref, wout_ref, g2_ref, up_hbm, fw_ref, fb_ref, down_ref, gf_ref) = params
    olc_ref, oh_ref, opb_ref, ofb_ref = state_out
    arena, ab, cx, cp, hst, fcar, hb, up_ref, sem_up = scratch

    def up_copies():
        return [pltpu.make_async_copy(up_hbm.at[:, k * FF_CHUNK:(k + 1) * FF_CHUNK], up_ref.at[k], sem_up.at[k])
                for k in range(2 * N_CHUNKS)]

    def start_up():
        for c in up_copies():
            c.start()

    def wait_up():
        for c in up_copies():
            c.wait()

    pl.when(i == 0)(start_up)
    rows = nb * tt
    lc = (LRU_CONV - 1) * nb
    pc = POOL_BUF * nb
    fc = (FFN_CONV - 1) * nb
    ex0 = 0
    ep0 = ex0 + lc + rows
    hl0 = ep0 + pc + rows
    xc0 = hl0 + rows

    def load_state():
        cx[...] = slc_ref[...]
        cp[...] = spb_ref[...]
        hst[...] = sh_ref[...]
        for k in range(2 * N_CHUNKS):
            fcar[k] = sfb_ref[:, k * FF_CHUNK:(k + 1) * FF_CHUNK]

    if sequential:
        pl.when(i == 0)(load_state)
    else:
        load_state()
    after_state_read()

    hb[...] = _rmsnorm(x_ref[...], g1_ref[...]).astype(BF16)
    arena[ex0:ex0 + lc] = cx[...]
    arena[ep0:ep0 + pc] = cp[...]
    arena[ex0 + lc:ex0 + lc + rows] = _mm(hb[...], win_ref[:, 0:D_LRU])
    arena[hl0:hl0 + rows] = _gelu_tanh(_mm(hb[...], win_ref[:, D_LRU:2 * D_LRU]))
    arena[ep0 + pc:ep0 + pc + rows] = _mm(hb[...], win_ref[:, 2 * D_LRU:])
    cx[...] = arena[ex0 + rows:ex0 + rows + lc]
    cp[...] = arena[ep0 + rows:ep0 + rows + pc]

    xc = cb_ref[...] + cw_ref[0:1, :] * arena[ex0:ex0 + rows]
    for j in range(1, LRU_CONV):
        xc = xc + cw_ref[j:j + 1, :] * arena[ex0 + j * nb:ex0 + j * nb + rows]
    arena[xc0:xc0 + rows] = xc

    softplus_neg_lam = jnp.logaddexp(-lam_ref[...], 0.0)
    for k in range(D_LRU // MXU_TILE):
        cs = slice(k * MXU_TILE, (k + 1) * MXU_TILE)
        xck = arena[xc0:xc0 + rows, cs]
        pre = _mm(xck.astype(BF16), wg_ref[k])
        r = jax.nn.sigmoid(pre[:, :MXU_TILE] + ba_ref[:, cs])
        ig = jax.nn.sigmoid(pre[:, MXU_TILE:] + bx_ref[:, cs])
        log_a = -LRU_C * r * softplus_neg_lam[:, cs]
        a = jnp.exp(log_a)
        m2 = jnp.tanh(-log_a) * (1.0 + a * a)
        mult = jnp.where(m2 > 0.0, m2 * lax.rsqrt(m2), 0.0)
        ab[:, k * MXU_TILE:(k + 1) * MXU_TILE] = a
        ab[:, D_LRU + k * MXU_TILE:D_LRU + (k + 1) * MXU_TILE] = mult * (ig * xck)

    h = hst[...]
    for t in range(tt):
        r0 = t * nb
        h = ab[r0:r0 + nb, 0:D_LRU] * h + ab[r0:r0 + nb, D_LRU:2 * D_LRU]
        arena[hl0 + r0:hl0 + r0 + nb, :] = h * arena[hl0 + r0:hl0 + r0 + nb, :]
    hst[...] = h

    windows_full = start >= POOL_BUF and not sequential
    if not windows_full:
        t_local = lax.shift_right_logical(lax.broadcasted_iota(jnp.int32, (rows, LANES), 0), nb.bit_length() - 1)
        pos = start + t_local + (i * tt if sequential else 0)
    pooled = []
    for g, w in enumerate(POOL_WINDOWS):
        assert w & (w - 1) == 0
        cs = slice(g * POOL_GROUP_DIM, (g + 1) * POOL_GROUP_DIM)
        cur = arena[ep0 + pc:ep0 + pc + rows, cs]
        s = arena[ep0 + pc - (w - 1) * nb:ep0 + pc + rows, cs]
        span = 1
        while span < w:
            s = s[span * nb:] + s[:s.shape[0] - span * nb]
            span *= 2
        mean = s * (1.0 / w) if windows_full else s / jnp.minimum(w, pos + 1).astype(F32)
        pooled.append((mean - cur).astype(BF16))

    x1 = x_ref[...] + _mm(arena[hl0:hl0 + rows].astype(BF16), wout_ref[0:D_LRU, :])
    for k in range(D_POOL // MXU_TILE):
        cs = slice(k * MXU_TILE, (k + 1) * MXU_TILE)
        pk = jnp.concatenate(pooled[2 * k:2 * k + 2], axis=-1)
        pool_out = _mm(pk, wp_ref[k]) * ps_ref[:, cs]
        x1 = x1 + _mm(pool_out.astype(BF16), wout_ref[D_LRU + k * MXU_TILE:D_LRU + (k + 1) * MXU_TILE, :])
    ab[...] = x1

    hb[...] = _rmsnorm(ab[...], g2_ref[...]).astype(BF16)

    after_x_read()
    pl.when(i == 0)(wait_up)

    def ffn_step(p, carry):
        down = None
        for q in range(CHUNKS_PER_STEP):
            c = p * CHUNKS_PER_STEP + q
            branches = []
            for j in range(2):
                base = (2 * q + j) * (fc + rows)
                idx = j * N_CHUNKS + c
                arena[base:base + fc] = fcar[idx]
                arena[base + fc:base + fc + rows] = _mm(hb[...], up_ref[idx])
                fcar[idx] = arena[base + rows:base + rows + fc]
                w = fw_ref[idx]
                u = fb_ref[idx] + w[0:1, :] * arena[base:base + rows]
                for t in range(1, FFN_CONV):
                    u = u + w[t:t + 1, :] * arena[base + t * nb:base + t * nb + rows]
                branches.append(u)
            act = (_gelu_tanh(branches[0]) * branches[1]).astype(BF16)
            d = _mm(act, down_ref[c])
            down = d if down is None else down + d
        ab[...] += down
        return carry

    lax.fori_loop(0, N_CHUNKS // CHUNKS_PER_STEP, ffn_step, 0)

    before_y_write()
    y_ref[...] = _rmsnorm(ab[...], gf_ref[...])

    olc_ref[...] = cx[...]
    opb_ref[...] = cp[...]
    oh_ref[...] = hst[...]
    for k in range(2 * N_CHUNKS):
        ofb_ref[:, k * FF_CHUNK:(k + 1) * FF_CHUNK] = fcar[k]


def _split_refs(refs):
    state_in = refs[:N_STATES]
    params = refs[N_STATES:N_STATES + N_PARAMS]
    return state_in, params, refs[N_STATES + N_PARAMS:]


def _group_kernel(x_hbm, *refs, nb, tt, start):
    state_hbm, params, rest = _split_refs(refs)
    y_hbm, state_out_hbm = rest[0], rest[1:1 + N_STATES]
    n_io = 1 + N_STATES
    stage_in = rest[1 + N_STATES:1 + N_STATES + n_io]
    stage_out = rest[1 + N_STATES + n_io:1 + N_STATES + 2 * n_io]
    sem_in, sem_out = rest[1 + N_STATES + 2 * n_io:3 + N_STATES + 2 * n_io]
    scratch = rest[3 + N_STATES + 2 * n_io:]
    g = pl.program_id(0)
    n = pl.num_programs(0)

    def copies(hbm, vmem, group, to_vmem, sem, batch_major=False):
        b0 = pl.multiple_of(group * nb, nb)
        if len(hbm.shape) == 2:
            pairs = [(hbm.at[pl.ds(b0, nb), :], vmem)]
        elif batch_major:
            pairs = [(hbm.at[pl.ds(b0, nb), k, :], vmem.at[pl.ds(k * nb, nb), :]) for k in range(hbm.shape[1])]
        else:
            pairs = [(hbm.at[k, pl.ds(b0, nb), :], vmem.at[pl.ds(k * nb, nb), :]) for k in range(hbm.shape[0])]
        return [pltpu.make_async_copy(h, v, sem) if to_vmem else pltpu.make_async_copy(v, h, sem) for h, v in pairs]

    def x_copies(group):
        return copies(x_hbm, stage_in[0], group, True, sem_in.at[0], batch_major=True)

    def state_copies(group):
        return [c for j in range(N_STATES)
                for c in copies(state_hbm[j], stage_in[1 + j], group, True, sem_in.at[1 + j])]

    def out_copies(group):
        cs = copies(y_hbm, stage_out[0], group, False, sem_out.at[0], batch_major=True)
        for j in range(N_STATES):
            cs += copies(state_out_hbm[j], stage_out[1 + j], group, False, sem_out.at[1 + j])
        return cs

    def start_all(cs):
        for c in cs:
            c.start()

    def wait_all(cs):
        for c in cs:
            c.wait()

    pl.when(g == 0)(lambda: start_all(state_copies(0) + x_copies(0)))
    wait_all(state_copies(g) + x_copies(g))

    def after_state_read():
        pl.when(g + 1 < n)(lambda: start_all(state_copies(g + 1)))

    def after_x_read():
        pl.when(g + 1 < n)(lambda: start_all(x_copies(g + 1)))

    def before_y_write():
        pl.when(g > 0)(lambda: wait_all(out_copies(g - 1)))

    _layer_body(g, stage_in[0], stage_out[0], stage_in[1:], params, stage_out[1:], scratch,
                nb=nb, tt=tt, sequential=False, start=start,
                after_state_read=after_state_read, after_x_read=after_x_read, before_y_write=before_y_write)

    start_all(out_copies(g))
    pl.when(g == n - 1)(lambda: wait_all(out_copies(g)))


def _sequence_kernel(x_hbm, *refs, nb, tt, start):
    state_in, params, rest = _split_refs(refs)
    y_hbm, state_out = rest[0], rest[1:1 + N_STATES]
    xbuf, ybuf, sem_x, sem_y = rest[1 + N_STATES:5 + N_STATES]
    scratch = rest[5 + N_STATES:]
    i = pl.program_id(0)
    n = pl.num_programs(0)

    def x_copies(step):
        t0 = pl.multiple_of(step * tt, tt)
        dst = xbuf.reshape(tt, nb, D_MODEL)
        return [pltpu.make_async_copy(x_hbm.at[b, pl.ds(t0, tt), :], dst.at[:, b, :], sem_x.at[b]) for b in range(nb)]

    def y_copies(step):
        t0 = pl.multiple_of(step * tt, tt)
        src = ybuf.reshape(tt, nb, D_MODEL)
        return [pltpu.make_async_copy(src.at[:, b, :], y_hbm.at[b, pl.ds(t0, tt), :], sem_y.at[b]) for b in range(nb)]

    def start_all(copies):
        for c in copies:
            c.start()

    def wait_all(copies):
        for c in copies:
            c.wait()

    pl.when(i == 0)(lambda: start_all(x_copies(0)))
    wait_all(x_copies(i))

    def before_y_write():
        pl.when(i > 0)(lambda: wait_all(y_copies(i - 1)))

    def after_x_read():
        pl.when(i + 1 < n)(lambda: start_all(x_copies(i + 1)))

    _layer_body(i, xbuf, ybuf, state_in, params, state_out, scratch,
                nb=nb, tt=tt, sequential=True, start=start,
                before_y_write=before_y_write, after_x_read=after_x_read)

    start_all(y_copies(i))
    pl.when(i == n - 1)(lambda: wait_all(y_copies(i)))


def _const_spec(shape):
    nd = len(shape)
    return pl.BlockSpec(shape, lambda i: (0,) * nd, pipeline_mode=pl.Buffered(1))


def _param_specs(params):
    return [pl.BlockSpec(memory_space=pl.ANY) if k == UP_PARAM else _const_spec(p.shape)
            for k, p in enumerate(params)]


def _state_shapes(nb):
    return [((LRU_CONV - 1) * nb, D_LRU), (nb, D_LRU), (POOL_BUF * nb, D_POOL), ((FFN_CONV - 1) * nb, 2 * D_FF)]


def _work_scratch(nb, tt):
    rows = nb * tt
    lc, pc, fc = (LRU_CONV - 1) * nb, POOL_BUF * nb, (FFN_CONV - 1) * nb
    arena_rows = max(4 * rows + lc + pc, 2 * CHUNKS_PER_STEP * (fc + rows))
    return [
        pltpu.VMEM((arena_rows, ARENA_W), F32),
        pltpu.VMEM((rows, D_MODEL), F32),
        pltpu.VMEM((lc, D_LRU), F32),
        pltpu.VMEM((pc, D_POOL), F32),
        pltpu.VMEM((nb, D_LRU), F32),
        pltpu.VMEM((2 * N_CHUNKS, fc, FF_CHUNK), F32),
        pltpu.VMEM((rows, D_MODEL), BF16),
        pltpu.VMEM((2 * N_CHUNKS, D_MODEL, FF_CHUNK), BF16),
        pltpu.SemaphoreType.DMA((2 * N_CHUNKS,)),
    ]


def _run_groups(x, states, params, *, groups, start, name):
    b, tt, _ = x.shape
    nb = b // groups
    assert nb % SUBLANES == 0 and nb & (nb - 1) == 0 and nb * groups == b and tt % SUBLANES == 0
    arrays = (x,) + tuple(states)
    any_spec = pl.BlockSpec(memory_space=pl.ANY)
    stage_shapes = [(nb * tt, D_MODEL)] + _state_shapes(nb)
    io_scratch = [pltpu.VMEM(s, F32) for s in stage_shapes]
    io_scratch += [pltpu.VMEM(s, F32) for s in stage_shapes]
    io_scratch += [pltpu.SemaphoreType.DMA((len(arrays),)), pltpu.SemaphoreType.DMA((len(arrays),))]
    return pl.pallas_call(
        functools.partial(_group_kernel, nb=nb, tt=tt, start=start),
        grid=(groups,),
        in_specs=[any_spec] * len(arrays) + _param_specs(params),
        out_specs=[any_spec] * len(arrays),
        out_shape=[jax.ShapeDtypeStruct(a.shape, F32) for a in arrays],
        scratch_shapes=io_scratch + _work_scratch(nb, tt),
        compiler_params=pltpu.CompilerParams(
            dimension_semantics=("arbitrary",), vmem_limit_bytes=VMEM_LIMIT_BYTES),
        name=name,
    )(*arrays, *params)


def _run_sequences(x, states, params, *, tt, start, name):
    nb, s, _ = x.shape
    rows = nb * tt
    assert nb % SUBLANES == 0 and nb & (nb - 1) == 0 and s % tt == 0
    state_map = lambda i: (0, 0)
    state_shapes = _state_shapes(nb)
    in_specs = [pl.BlockSpec(memory_space=pl.ANY)]
    in_specs += [pl.BlockSpec(sh, state_map) for sh in state_shapes]
    in_specs += _param_specs(params)
    out_specs = [pl.BlockSpec(memory_space=pl.ANY)]
    out_specs += [pl.BlockSpec(sh, state_map) for sh in state_shapes]
    out_shape = [jax.ShapeDtypeStruct(x.shape, F32)]
    out_shape += [jax.ShapeDtypeStruct(sh, F32) for sh in state_shapes]
    io_scratch = [
        pltpu.VMEM((rows, D_MODEL), F32),
        pltpu.VMEM((rows, D_MODEL), F32),
        pltpu.SemaphoreType.DMA((nb,)),
        pltpu.SemaphoreType.DMA((nb,)),
    ]
    return pl.pallas_call(
        functools.partial(_sequence_kernel, nb=nb, tt=tt, start=start),
        grid=(s // tt,),
        in_specs=in_specs,
        out_specs=out_specs,
        out_shape=out_shape,
        scratch_shapes=io_scratch + _work_scratch(nb, tt),
        compiler_params=pltpu.CompilerParams(
            dimension_semantics=("arbitrary",), vmem_limit_bytes=VMEM_LIMIT_BYTES),
        name=name,
    )(x, *states, *params)


def _block_diag_tiles(w, per_tile):
    n, d, _ = w.shape
    tiles = []
    for k in range(n // per_tile):
        tiles.append(jax.scipy.linalg.block_diag(*[w[k * per_tile + j] for j in range(per_tile)]))
    return jnp.stack(tiles)


def _chunk_columns(w):
    k = w.shape[0]
    return w.reshape(k, 2 * N_CHUNKS, FF_CHUNK).transpose(1, 0, 2)


def _to_time_major(a, groups):
    b, k, c = a.shape
    return a.reshape(groups, b // groups, k, c).transpose(0, 2, 1, 3).reshape(b * k, c)


def _from_time_major(a, groups, b, k):
    c = a.shape[-1]
    return a.reshape(groups, k, b // groups, c).transpose(0, 2, 1, 3).reshape(b, k, c)


PROMPT_TT = 128
SAMPLE_GROUPS = 2


def kernel(x_prompt, x_sample, state_lru_conv, state_lru_h, state_pool, state_ffn_conv, norm1_g, w_in, lru_conv_w, lru_conv_b, lru_wa, lru_ba, lru_wx, lru_bx, lru_lambda, pool_w, pool_scale, w_out, norm2_g, ffn_up, ffn_conv_w, ffn_conv_b, ffn_down, final_g):
    depth = w_in.shape[0]
    assert depth == 1
    l = 0
    row = lambda v: v.reshape(1, -1)
    per_tile = MXU_TILE // LRU_HEAD_DIM
    wa_t = _block_diag_tiles(lru_wa[l], per_tile)
    wx_t = _block_diag_tiles(lru_wx[l], per_tile)
    params = (
        row(norm1_g[l]), w_in[l].astype(BF16), lru_conv_w[l], row(lru_conv_b[l]),
        jnp.concatenate([wa_t, wx_t], axis=-1).astype(BF16),
        row(lru_ba[l]), row(lru_bx[l]), row(lru_lambda[l]),
        _block_diag_tiles(pool_w[l], MXU_TILE // POOL_GROUP_DIM).astype(BF16), row(pool_scale[l]),
        w_out[l].astype(BF16), row(norm2_g[l]), ffn_up[l].astype(BF16),
        _chunk_columns(ffn_conv_w[l]), _chunk_columns(row(ffn_conv_b[l])),
        ffn_down[l].astype(BF16).reshape(N_CHUNKS, FF_CHUNK, D_MODEL), row(final_g),
    )
    assert len(params) == N_PARAMS

    bp, sp, _ = x_prompt.shape
    p_states = tuple(jnp.zeros(sh, F32) for sh in _state_shapes(bp))
    yp, p_lc, p_h, p_pb, p_fb = _run_sequences(
        x_prompt, p_states, params, tt=PROMPT_TT, start=0, name="layer_prompt")

    step_major = lambda a: a.transpose(1, 0, 2)
    s_states = (step_major(state_lru_conv[l]), state_lru_h[l], step_major(state_pool[l]), step_major(state_ffn_conv[l]))
    ys, s_lc, s_h, s_pb, s_fb = _run_groups(
        x_sample, s_states, params, groups=SAMPLE_GROUPS, start=PAST_LEN, name="layer_sample")

    out = (
        yp, ys,
        _from_time_major(p_lc, 1, bp, LRU_CONV - 1)[None], p_h[None],
        _from_time_major(p_pb, 1, bp, POOL_BUF)[None], _from_time_major(p_fb, 1, bp, FFN_CONV - 1)[None],
        step_major(s_lc)[None], s_h[None], step_major(s_pb)[None], step_major(s_fb)[None],
    )
    return out
```
